```python
import math
import jax
import jax.numpy as jnp
from jax import lax
import numpy as np

D_MODEL = 1024
BATCH = 8
SEQ = 2048
DEPTH = 4
DEC_BATCH = 128
DEC_SEQ = 4
PAST_LEN = 8192
PAGE_SIZE = 128

D_MIX = D_MODEL
HEAD_DIM = 64
D_ATTN = D_MIX // 2
N_Q_HEADS = D_ATTN // HEAD_DIM
N_KV_HEADS = max(1, N_Q_HEADS // 4)
Q_PER_KV = N_Q_HEADS // N_KV_HEADS
D_KV = N_KV_HEADS * HEAD_DIM
WINDOW = 128
ROPE_THETA = 500000.0
ROT_DIM = HEAD_DIM // 4
D_SSM = D_MIX // 4
SSM_GROUP = 16
N_SSM_GROUPS = D_SSM // SSM_GROUP
SSM_STATE = 64
D_RET = D_MIX - D_ATTN - D_SSM
N_RET_HEADS = D_RET // HEAD_DIM
RET_THETA = 10000.0
RET_CHUNK = 128
PLE_DIM = 256
EPS = 1e-6
SPLIT_WIDTHS = (D_ATTN, D_KV, D_KV, D_ATTN, D_SSM, D_SSM, D_RET, D_RET, D_RET, D_RET)
SPLIT_POINTS = tuple(int(s) for s in np.cumsum(SPLIT_WIDTHS)[:-1])
D_IN = sum(SPLIT_WIDTHS)

kernel_name = 'hymba_swa_s5_retention_decode_step'

F32 = jnp.float32


def rmsnorm(x, g):
    x32 = x.astype(F32)
    y = x32 * lax.rsqrt(jnp.mean(x32 * x32, axis=-1, keepdims=True) + EPS)
    return (y * g.astype(F32)).astype(x.dtype)


def rope(x, pos, theta, rot_dim):
    half = rot_dim // 2
    inv = 1.0 / jnp.power(theta, jnp.arange(half, dtype=F32) / half)
    ang = pos[:, None] * inv[None, :]
    cos = jnp.cos(ang)[None, :, None, :]
    sin = jnp.sin(ang)[None, :, None, :]
    xr = x[..., :rot_dim].astype(F32)
    x1, x2 = xr[..., :half], xr[..., half:]
    rot = jnp.concatenate([x1 * cos - x2 * sin, x2 * cos + x1 * sin], axis=-1).astype(x.dtype)
    return jnp.concatenate([rot, x[..., rot_dim:]], axis=-1)


def swa_sink_attention(q, k, v, win_k, win_v, sink, pos0):
    B, T = q.shape[0], q.shape[1]
    bq = min(WINDOW, T)
    nb = T // bq
    L = WINDOW + bq
    k_all = jnp.concatenate([win_k.astype(k.dtype), k], axis=1)
    v_all = jnp.concatenate([win_v.astype(v.dtype), v], axis=1)
    idx = (jnp.arange(nb) * bq)[:, None] + jnp.arange(L)[None, :]
    kb = k_all[:, idx].astype(F32)
    vb = v_all[:, idx].astype(F32)
    qb = q.reshape(B, nb, bq, N_KV_HEADS, Q_PER_KV, HEAD_DIM).astype(F32)
    s = jnp.einsum('bnikgd,bnlkd->bnkgil', qb, kb) * (HEAD_DIM ** -0.5)
    i = jnp.arange(bq)[:, None]
    l = jnp.arange(L)[None, :]
    diff = i + WINDOW - l
    kpos = pos0 - WINDOW + (jnp.arange(nb) * bq)[:, None, None] + l[None]
    valid = (diff >= 0)[None] & (diff < WINDOW)[None] & (kpos >= 0)
    s = jnp.where(valid[None, :, None, None], s, -jnp.inf)
    sk = sink.astype(F32).reshape(N_KV_HEADS, Q_PER_KV)[None, None, :, :, None, None]
    m = jnp.maximum(jnp.max(s, axis=-1, keepdims=True), sk)
    e = jnp.exp(s - m)
    p = e / (jnp.sum(e, axis=-1, keepdims=True) + jnp.exp(sk - m))
    o = jnp.einsum('bnkgil,bnlkd->bnikgd', p, vb).reshape(B, T, D_ATTN)
    return o, k_all[:, -WINDOW:], v_all[:, -WINDOW:]


def _ssm_combine(e1, e2):
    a1r, a1i, b1r, b1i = e1
    a2r, a2i, b2r, b2i = e2
    return (a2r * a1r - a2i * a1i,
            a2r * a1i + a2i * a1r,
            a2r * b1r - a2i * b1i + b2r,
            a2r * b1i + a2i * b1r + b2i)


def s5_layer(u, s0_re, s0_im, lam_re, lam_im, log_dt, b_re, b_im, c_re, c_im, d_skip):
    B, T = u.shape[0], u.shape[1]
    u = u.astype(F32).reshape(B, T, N_SSM_GROUPS, SSM_GROUP)
    lr, li = lam_re.astype(F32), lam_im.astype(F32)
    dt = jnp.exp(log_dt.astype(F32))[:, None]
    mag = jnp.exp(lr * dt)
    a_re, a_im = mag * jnp.cos(li * dt), mag * jnp.sin(li * dt)
    nr, ni = a_re - 1.0, a_im
    den = lr * lr + li * li
    coef_re = (nr * lr + ni * li) / den
    coef_im = (ni * lr - nr * li) / den
    br, bi = b_re.astype(F32), b_im.astype(F32)
    bb_re = coef_re[..., None] * br - coef_im[..., None] * bi
    bb_im = coef_re[..., None] * bi + coef_im[..., None] * br
    bu_re = jnp.einsum('btgh,gph->btgp', u, bb_re)
    bu_im = jnp.einsum('btgh,gph->btgp', u, bb_im)
    s0r, s0i = s0_re.astype(F32), s0_im.astype(F32)
    bu_re = bu_re.at[:, 0].add(a_re * s0r - a_im * s0i)
    bu_im = bu_im.at[:, 0].add(a_re * s0i + a_im * s0r)
    aa_re = jnp.broadcast_to(a_re, bu_re.shape)
    aa_im = jnp.broadcast_to(a_im, bu_im.shape)
    _, _, s_re, s_im = lax.associative_scan(_ssm_combine, (aa_re, aa_im, bu_re, bu_im), axis=1)
    y = (jnp.einsum('btgp,ghp->btgh', s_re, c_re.astype(F32))
         - jnp.einsum('btgp,ghp->btgh', s_im, c_im.astype(F32))
         + d_skip.astype(F32) * u)
    return y.reshape(B, T, D_SSM), s_re[:, -1], s_im[:, -1]


def retention(q, k, v, r0):
    B, T = q.shape[0], q.shape[1]
    C = min(RET_CHUNK, T)
    N = T // C
    gammas = 1.0 - jnp.power(2.0, -5.0 - jnp.arange(N_RET_HEADS, dtype=F32))
    log_g = jnp.log(gammas)
    i = jnp.arange(C, dtype=F32)
    rel = i[:, None] - i[None, :]
    decay_mask = jnp.where(rel[None] >= 0, jnp.exp(log_g[:, None, None] * jnp.maximum(rel, 0.0)[None]), 0.0)
    qc = q.astype(F32).reshape(B, N, C, N_RET_HEADS, HEAD_DIM)
    kc = k.astype(F32).reshape(B, N, C, N_RET_HEADS, HEAD_DIM)
    vc = v.astype(F32).reshape(B, N, C, N_RET_HEADS, HEAD_DIM)
    sc = jnp.einsum('bnihd,bnjhd->bnhij', qc, kc) * decay_mask[None, None]
    intra = jnp.einsum('bnhij,bnjhe->bnihe', sc, vc)
    kdec = jnp.exp(log_g[None, :] * (C - 1.0 - i)[:, None])
    d_r = jnp.einsum('bnjhd,jh,bnjhe->bnhde', kc, kdec, vc)
    g_chunk = jnp.exp(log_g * C)[None, :, None, None]

    def step(r, d_rn):
        return g_chunk * r + d_rn, r

    r_final, r_in = lax.scan(step, r0.astype(F32), jnp.moveaxis(d_r, 1, 0))
    r_in = jnp.moveaxis(r_in, 0, 1)
    qdec = jnp.exp(log_g[None, :] * (i + 1.0)[:, None])
    cross = jnp.einsum('bnihd,bnhde->bnihe', qc, r_in) * qdec[None, None, :, :, None]
    return (intra + cross).reshape(B, T, N_RET_HEADS, HEAD_DIM), r_final


def mixer_layer(h, p_l, pos0, win_k, win_v, s_re, s_im, ret_r,
                norm_g, w_in, attn_sink, lam_re, lam_im, log_dt, b_re, b_im, c_re, c_im,
                d_skip, w_glu, ret_norm_g, w_out, w_ple_proj, w_ple_gate):
    B, T = h.shape[0], h.shape[1]
    hn = rmsnorm(h, norm_g)
    proj = hn @ w_in
    qa, ka, va, ga, ub, gb, qr, kr, vr, gr = jnp.split(proj, SPLIT_POINTS, axis=-1)
    pos = jnp.arange(T, dtype=F32) + pos0
    qa = rope(qa.reshape(B, T, N_Q_HEADS, HEAD_DIM), pos, ROPE_THETA, ROT_DIM)
    ka = rope(ka.reshape(B, T, N_KV_HEADS, HEAD_DIM), pos, ROPE_THETA, ROT_DIM)
    va = va.reshape(B, T, N_KV_HEADS, HEAD_DIM)
    oa, new_k, new_v = swa_sink_attention(qa, ka, va, win_k, win_v, attn_sink, pos0)
    ya = oa * jax.nn.silu(ga.astype(F32))
    yb, new_sre, new_sim = s5_layer(ub, s_re, s_im, lam_re, lam_im, log_dt, b_re, b_im, c_re, c_im, d_skip)
    yb = jax.nn.gelu(yb)
    yb = yb * jax.nn.sigmoid(yb @ w_glu.astype(F32))
    yb = yb * jax.nn.silu(gb.astype(F32))
    qr = rope(qr.reshape(B, T, N_RET_HEADS, HEAD_DIM), pos, RET_THETA, HEAD_DIM)
    kr = rope(kr.reshape(B, T, N_RET_HEADS, HEAD_DIM), pos, RET_THETA, HEAD_DIM) * (HEAD_DIM ** -0.5)
    vr = vr.reshape(B, T, N_RET_HEADS, HEAD_DIM)
    orr, new_r = retention(qr, kr, vr, ret_r)
    orr = orr * lax.rsqrt(jnp.mean(orr * orr, axis=-1, keepdims=True) + EPS)
    yc = orr.reshape(B, T, D_RET) * ret_norm_g.astype(F32) * jax.nn.silu(gr.astype(F32))
    y = jnp.concatenate([ya, yb, yc], axis=-1).astype(h.dtype) @ w_out
    h = h + y
    h = h + jax.nn.sigmoid(h @ w_ple_gate) * (p_l @ w_ple_proj)
    return h, new_k, new_v, new_sre, new_sim, new_r


def setup_inputs(seed: int = 0) -> dict:
    key = jax.random.key(seed)
    ks = jax.random.split(key, 32)

    def nrm(k, shape, scale):
        return jax.random.normal(k, shape, F32) * scale

    n_idx = jnp.arange(SSM_STATE, dtype=F32)
    return {
        'x_prompt': nrm(ks[0], (BATCH, SEQ, D_MODEL), 1.0),
        'x_sample': nrm(ks[1], (DEC_BATCH, DEC_SEQ, D_MODEL), 1.0),
        'p_prompt': nrm(ks[2], (DEPTH, BATCH, SEQ, PLE_DIM), 1.0),
        'p_sample': nrm(ks[3], (DEPTH, DEC_BATCH, DEC_SEQ, PLE_DIM), 1.0),
        'cache_win_k': nrm(ks[4], (DEPTH, DEC_BATCH, WINDOW, N_KV_HEADS, HEAD_DIM), 1.0),
        'cache_win_v': nrm(ks[5], (DEPTH, DEC_BATCH, WINDOW, N_KV_HEADS, HEAD_DIM), 1.0),
        'state_ssm_re': nrm(ks[6], (DEPTH, DEC_BATCH, N_SSM_GROUPS, SSM_STATE), 0.1),
        'state_ssm_im': nrm(ks[7], (DEPTH, DEC_BATCH, N_SSM_GROUPS, SSM_STATE), 0.1),
        'state_ret': nrm(ks[8], (DEPTH, DEC_BATCH, N_RET_HEADS, HEAD_DIM, HEAD_DIM), 0.5),
        'norm_g': 1.0 + nrm(ks[9], (DEPTH, D_MODEL), 0.02),
        'w_in': nrm(ks[10], (DEPTH, D_MODEL, D_IN), D_MODEL ** -0.5),
        'attn_sink': nrm(ks[11], (DEPTH, N_Q_HEADS), 0.5),
        'ssm_lam_re': -0.5 + nrm(ks[12], (DEPTH, N_SSM_GROUPS, SSM_STATE), 0.01),
        'ssm_lam_im': jnp.pi * n_idx + nrm(ks[13], (DEPTH, N_SSM_GROUPS, SSM_STATE), 0.01),
        'ssm_log_dt': jax.random.uniform(ks[14], (DEPTH, N_SSM_GROUPS), F32, math.log(1e-3), math.log(1e-1)),
        'ssm_b_re': nrm(ks[15], (DEPTH, N_SSM_GROUPS, SSM_STATE, SSM_GROUP), (2 * SSM_GROUP) ** -0.5),
        'ssm_b_im': nrm(ks[16], (DEPTH, N_SSM_GROUPS, SSM_STATE, SSM_GROUP), (2 * SSM_GROUP) ** -0.5),
        'ssm_c_re': nrm(ks[17], (DEPTH, N_SSM_GROUPS, SSM_GROUP, SSM_STATE), (2 * SSM_STATE) ** -0.5),
        'ssm_c_im': nrm(ks[18], (DEPTH, N_SSM_GROUPS, SSM_GROUP, SSM_STATE), (2 * SSM_STATE) ** -0.5),
        'ssm_d': nrm(ks[19], (DEPTH, N_SSM_GROUPS, SSM_GROUP), 1.0),
        'w_glu': nrm(ks[20], (DEPTH, D_SSM, D_SSM), D_SSM ** -0.5),
        'ret_norm_g': 1.0 + nrm(ks[21], (DEPTH, D_RET), 0.02),
        'w_out': nrm(ks[22], (DEPTH, D_MIX, D_MODEL), D_MIX ** -0.5),
        'w_ple_proj': nrm(ks[23], (DEPTH, PLE_DIM, D_MODEL), PLE_DIM ** -0.5),
        'w_ple_gate': nrm(ks[24], (DEPTH, D_MODEL, D_MODEL), D_MODEL ** -0.5),
        'final_norm_g': 1.0 + nrm(ks[25], (D_MODEL,), 0.02),
    }


def reference(x_prompt, x_sample, p_prompt, p_sample, cache_win_k, cache_win_v,
              state_ssm_re, state_ssm_im, state_ret,
              norm_g, w_in, attn_sink, ssm_lam_re, ssm_lam_im, ssm_log_dt,
              ssm_b_re, ssm_b_im, ssm_c_re, ssm_c_im, ssm_d, w_glu, ret_norm_g,
              w_out, w_ple_proj, w_ple_gate, final_norm_g):
    bp = x_prompt.shape[0]
    dt = x_prompt.dtype
    z_k = jnp.zeros((bp, WINDOW, N_KV_HEADS, HEAD_DIM), dt)
    z_s = jnp.zeros((bp, N_SSM_GROUPS, SSM_STATE), F32)
    z_r = jnp.zeros((bp, N_RET_HEADS, HEAD_DIM, HEAD_DIM), F32)
    hp, hs = x_prompt, x_sample
    kp_l, vp_l, srp_l, sip_l, rp_l = [], [], [], [], []
    ks_l, vs_l, srs_l, sis_l, rs_l = [], [], [], [], []
    for i in range(DEPTH):
        lw = (norm_g[i], w_in[i], attn_sink[i], ssm_lam_re[i], ssm_lam_im[i], ssm_log_dt[i],
              ssm_b_re[i], ssm_b_im[i], ssm_c_re[i], ssm_c_im[i], ssm_d[i], w_glu[i],
              ret_norm_g[i], w_out[i], w_ple_proj[i], w_ple_gate[i])
        hp, kp, vp, srp, sip, rp = mixer_layer(hp, p_prompt[i], 0, z_k, z_k, z_s, z_s, z_r, *lw)
        hs, kk, vv, srs, sis, rs = mixer_layer(hs, p_sample[i], PAST_LEN, cache_win_k[i], cache_win_v[i],
                                               state_ssm_re[i], state_ssm_im[i], state_ret[i], *lw)
        kp_l.append(kp); vp_l.append(vp); srp_l.append(srp); sip_l.append(sip); rp_l.append(rp)
        ks_l.append(kk); vs_l.append(vv); srs_l.append(srs); sis_l.append(sis); rs_l.append(rs)
    y_prompt = rmsnorm(hp, final_norm_g)
    y_sample = rmsnorm(hs, final_norm_g)
    return (y_prompt, y_sample,
            jnp.stack(kp_l), jnp.stack(vp_l), jnp.stack(srp_l), jnp.stack(sip_l), jnp.stack(rp_l),
            jnp.stack(ks_l), jnp.stack(vs_l), jnp.stack(srs_l), jnp.stack(sis_l), jnp.stack(rs_l))
```

```python
import functools
import math

import numpy as np
import jax
import jax.numpy as jnp
from jax import lax
from jax.experimental import pallas as pl
from jax.experimental.pallas import tpu as pltpu

F32 = jnp.float32
BF16 = jnp.bfloat16

D_MODEL = 1024
HEAD_DIM = 64
D_ATTN = 512
N_Q_HEADS = 8
N_KV_HEADS = 2
D_KV = 128
WINDOW = 128
ROPE_THETA = 500000.0
ROT_DIM = 16
D_SSM = 256
SSM_GROUP = 16
N_SSM_GROUPS = 16
SSM_STATE = 64
D_STATE = N_SSM_GROUPS * SSM_STATE
D_RET = 256
N_RET_HEADS = 4
RET_THETA = 10000.0
PLE_DIM = 256
EPS = 1e-6
D_IN = 2816
COL_A = 0
COL_B = 1280
COL_C = 1792
LANES = 128
MXU_N = 256
ROW_GROUP = 256
VMEM_LIMIT = 58 * 1024 * 1024


def _mm(a, b):
    return jnp.dot(a.astype(BF16), b.astype(BF16), preferred_element_type=F32)


def _mm_nt(a, b):
    return lax.dot_general(a.astype(BF16), b.astype(BF16), (((1,), (1,)), ((), ())),
                           preferred_element_type=F32)


def _silu(x):
    return x * jax.nn.sigmoid(x)


def _layer_kernel(cfg,
                  h_ref, p_ref, rope_ref, dmask_ref, rdec_ref,
                  kwin_ref, vwin_ref, sre0_ref, sim0_ref, r0_ref,
                  sink_ref, ng_ref, win_ref, lam_ref, bbd_ref, wc_ref, d_ref, wglu_ref, rg_ref,
                  wout_ref, wpp_ref, wgate_ref, fg_ref,
                  hout_ref, kout_ref, vout_ref, sreo_ref, simo_ref, ro_ref,
                  kprev, vprev, sre, sim, rst, wb_s, a_s, hn_s, pa_s, pc_s, u_s, gb_s,
                  ya_s, yc_s, yb_s, ut_s, bu_s, h1_s, h1b_s):
    bblk, tc, tcp, n_chunks, pos0, final_norm = cfg
    gsz = ROW_GROUP // tcp
    n_groups = bblk // gsz
    sb = min(tc, ROW_GROUP // bblk)
    n_sub = tc // sb
    rows_sb = sb * bblk
    n = pl.program_id(1)

    @pl.when(n == 0)
    def _init():
        kprev[...] = kwin_ref[...]
        vprev[...] = vwin_ref[...]
        sre[...] = sre0_ref[...]
        sim[...] = sim0_ref[...]
        rst[...] = r0_ref[...]
        lr = lam_ref[0:1, :]
        li = lam_ref[1:2, :]
        dt = jnp.exp(lam_ref[2:3, :])
        mag = jnp.exp(lr * dt)
        a_re = mag * jnp.cos(li * dt)
        a_im = mag * jnp.sin(li * dt)
        nr = a_re - 1.0
        ni = a_im
        den = lr * lr + li * li
        coef_re = (nr * lr + ni * li) / den
        coef_im = (ni * lr - nr * li) / den
        a_s[0:1, :] = a_re
        a_s[1:2, :] = a_im
        br = bbd_ref[0]
        bi = bbd_ref[1]
        wb_s[:, 0:D_STATE] = (coef_re * br - coef_im * bi).astype(BF16)
        wb_s[:, D_STATE:2 * D_STATE] = (coef_re * bi + coef_im * br).astype(BF16)
        if tc < tcp:
            yb_s[...] = jnp.zeros_like(yb_s)

    lane = lax.broadcasted_iota(jnp.int32, (tcp, LANES), 1)
    lo_q = lane < HEAD_DIM
    lane_w = lax.broadcasted_iota(jnp.int32, (WINDOW, LANES), 1)
    lo_w = lane_w < HEAD_DIM
    row_w = lax.broadcasted_iota(jnp.int32, (WINDOW, LANES), 0)
    blockdiag = (row_w < HEAD_DIM) == lo_w
    headsum = jnp.where(blockdiag, 1.0 / HEAD_DIM, 0.0).astype(BF16)

    cos_a, sa_hi, sa_lo = rope_ref[0], rope_ref[1], rope_ref[2]
    cos_r, sr_hi, sr_lo = rope_ref[3], rope_ref[4], rope_ref[5]

    def rope_a(x):
        return (x * cos_a + pltpu.roll(x, ROT_DIM // 2, 1) * sa_hi
                + pltpu.roll(x, LANES - ROT_DIM // 2, 1) * sa_lo)

    def rope_r(x):
        return (x * cos_r + pltpu.roll(x, HEAD_DIM // 2, 1) * sr_hi
                + pltpu.roll(x, LANES - HEAD_DIM // 2, 1) * sr_lo)

    def kv_variants(x, lo):
        xr = pltpu.roll(x, HEAD_DIM, 1)
        return (jnp.where(lo, x, 0.0), jnp.where(lo, 0.0, xr),
                jnp.where(lo, xr, 0.0), jnp.where(lo, 0.0, x))

    qi_p = lax.broadcasted_iota(jnp.int32, (tcp, WINDOW), 0)
    kl_p = lax.broadcasted_iota(jnp.int32, (tcp, WINDOW), 1)
    lmin = WINDOW - pos0 - n * tc
    valid_p = (kl_p > qi_p) & (kl_p >= lmin)
    qi_c = lax.broadcasted_iota(jnp.int32, (tcp, tcp), 0)
    kl_c = lax.broadcasted_iota(jnp.int32, (tcp, tcp), 1)
    valid_c = kl_c <= qi_c

    def group_body(g, carry):
        b0 = pl.multiple_of(g * gsz, gsz)
        r0 = pl.multiple_of(g * ROW_GROUP, ROW_GROUP)
        x = h_ref[pl.ds(b0, gsz)].reshape(ROW_GROUP, D_MODEL)
        ms = jnp.mean(x * x, axis=-1, keepdims=True)
        hn_s[...] = (x * lax.rsqrt(ms + EPS) * ng_ref[...]).astype(BF16)
        for c in range(0, COL_B - COL_A, MXU_N):
            pa_s[:, c:c + MXU_N] = jnp.dot(hn_s[...], win_ref[:, COL_A + c:COL_A + c + MXU_N],
                                           preferred_element_type=F32)
        for c in range(0, D_IN - COL_C, MXU_N):
            pc_s[:, c:c + MXU_N] = jnp.dot(hn_s[...], win_ref[:, COL_C + c:COL_C + c + MXU_N],
                                           preferred_element_type=F32)
        ub = jnp.dot(hn_s[...], win_ref[:, COL_B:COL_B + D_SSM], preferred_element_type=F32)
        u_s[0, pl.ds(r0, ROW_GROUP), :] = ub[:, 0:LANES]
        u_s[1, pl.ds(r0, ROW_GROUP), :] = ub[:, LANES:2 * LANES]
        gb_s[pl.ds(r0, ROW_GROUP), :] = jnp.dot(hn_s[...], win_ref[:, COL_B + D_SSM:COL_C],
                                                preferred_element_type=F32)

        def batch_body(j, carry2):
            b = b0 + j
            rj = pl.multiple_of(j * tcp, tcp)
            rb = pl.multiple_of(b * tcp, tcp)
            k_cur = rope_a(pa_s[pl.ds(rj, tcp), D_ATTN:D_ATTN + D_KV])
            v_cur = pa_s[pl.ds(rj, tcp), D_ATTN + D_KV:D_ATTN + 2 * D_KV]
            kout_ref[b] = k_cur
            vout_ref[b] = v_cur
            kp_var = kv_variants(kprev[b], lo_w)
            vp_var = kv_variants(vprev[b], lo_w)
            kc_var = kv_variants(k_cur, lo_q)
            vc_var = kv_variants(v_cur, lo_q)
            for pj in range(N_Q_HEADS // 2):
                qp = rope_a(pa_s[pl.ds(rj, tcp), pj * LANES:(pj + 1) * LANES])
                o = jnp.zeros((tcp, LANES), F32)
                for e in range(2):
                    vi = 2 * (pj // 2) + e
                    s_p = _mm_nt(qp, kp_var[vi]) * (HEAD_DIM ** -0.5)
                    s_c = _mm_nt(qp, kc_var[vi]) * (HEAD_DIM ** -0.5)
                    s_p = jnp.where(valid_p, s_p, -jnp.inf)
                    s_c = jnp.where(valid_c, s_c, -jnp.inf)
                    sk = sink_ref[0, 2 * pj + e]
                    m = jnp.maximum(jnp.maximum(jnp.max(s_p, axis=-1, keepdims=True),
                                                jnp.max(s_c, axis=-1, keepdims=True)), sk)
                    e_p = jnp.exp(s_p - m)
                    e_c = jnp.exp(s_c - m)
                    den = (jnp.sum(e_p, axis=-1, keepdims=True) + jnp.sum(e_c, axis=-1, keepdims=True)
                           + jnp.exp(sk - m))
                    inv = 1.0 / den
                    o = o + _mm(e_p * inv, vp_var[vi]) + _mm(e_c * inv, vc_var[vi])
                ga = pa_s[pl.ds(rj, tcp), D_ATTN + 2 * D_KV + pj * LANES:D_ATTN + 2 * D_KV + (pj + 1) * LANES]
                ya_s[pl.ds(rb, tcp), pj * LANES:(pj + 1) * LANES] = o * _silu(ga)
            if n_chunks > 1:
                kprev[b] = k_cur
                vprev[b] = v_cur
            for pr in range(N_RET_HEADS // 2):
                cs = slice(pr * LANES, (pr + 1) * LANES)
                q = rope_r(pc_s[pl.ds(rj, tcp), pr * LANES:(pr + 1) * LANES])
                k = rope_r(pc_s[pl.ds(rj, tcp), D_RET + pr * LANES:D_RET + (pr + 1) * LANES]) * (HEAD_DIM ** -0.5)
                v = pc_s[pl.ds(rj, tcp), 2 * D_RET + pr * LANES:2 * D_RET + (pr + 1) * LANES]
                gr = pc_s[pl.ds(rj, tcp), 3 * D_RET + pr * LANES:3 * D_RET + (pr + 1) * LANES]
                r_old = rst[b, pr]
                o = _mm(q, r_old) * rdec_ref[1, :, cs]
                for e in range(2):
                    mh = lo_q if e == 0 else jnp.logical_not(lo_q)
                    sc = _mm_nt(q, jnp.where(mh, k, 0.0)) * dmask_ref[2 * pr + e]
                    o = o + _mm(sc, jnp.where(mh, v, 0.0))
                kd = k * rdec_ref[0, :, cs]
                if tcp < WINDOW:
                    pad = jnp.zeros((WINDOW - tcp, LANES), F32)
                    d_new = _mm(jnp.concatenate([kd, pad], axis=0).T, jnp.concatenate([v, pad], axis=0))
                else:
                    d_new = _mm(kd.T, v)
                rst[b, pr] = r_old * rdec_ref[2, 0:1, cs] + jnp.where(blockdiag, d_new, 0.0)
                o2 = o * o
                o2_hi = o2.astype(BF16)
                o2_lo = (o2 - o2_hi.astype(F32)).astype(BF16)
                msq = (jnp.dot(o2_hi, headsum, preferred_element_type=F32)
                       + jnp.dot(o2_lo, headsum, preferred_element_type=F32))
                yc_s[pl.ds(rb, tcp), pr * LANES:(pr + 1) * LANES] = (
                    o * lax.rsqrt(msq + EPS) * rg_ref[:, cs] * _silu(gr))
            return carry2

        lax.fori_loop(0, gsz, batch_body, 0)
        return carry

    lax.fori_loop(0, n_groups, group_body, 0)

    a_re_b = jnp.broadcast_to(a_s[0:1, :], (bblk, D_STATE))
    a_im_b = jnp.broadcast_to(a_s[1:2, :], (bblk, D_STATE))

    def sub_body(si, carry):
        t0 = si * sb
        for t in range(sb):
            for s in range(2):
                ut_s[t * bblk:(t + 1) * bblk, s * LANES:(s + 1) * LANES] = (
                    u_s[s, pl.ds(t0 + t, bblk, stride=tcp), :])
        for c in range(0, 2 * D_STATE, MXU_N):
            bu_s[0:rows_sb, c:c + MXU_N] = _mm(ut_s[0:rows_sb, :], wb_s[:, c:c + MXU_N])

        def step(t, xs):
            xr, xi = xs
            rt = pl.multiple_of(t * bblk, bblk)
            b_re = bu_s[pl.ds(rt, bblk), 0:D_STATE]
            b_im = bu_s[pl.ds(rt, bblk), D_STATE:2 * D_STATE]
            nxr = a_re_b * xr - a_im_b * xi + b_re
            nxi = a_re_b * xi + a_im_b * xr + b_im
            bu_s[pl.ds(rt, bblk), 0:D_STATE] = nxr
            bu_s[pl.ds(rt, bblk), D_STATE:2 * D_STATE] = nxi
            return nxr, nxi

        xr, xi = lax.fori_loop(0, sb, step, (sre[...], sim[...]))
        sre[...] = xr
        sim[...] = xi
        y = (_mm(bu_s[0:rows_sb, 0:D_STATE], wc_ref[0]) - _mm(bu_s[0:rows_sb, D_STATE:2 * D_STATE], wc_ref[1])
             + d_ref[...] * ut_s[0:rows_sb, :])
        y = jax.nn.gelu(y)
        y = y * jax.nn.sigmoid(_mm(y, wglu_ref[...]))
        for t in range(sb):
            for s in range(2):
                yb_s[s, pl.ds(t0 + t, bblk, stride=tcp), :] = y[t * bblk:(t + 1) * bblk, s * LANES:(s + 1) * LANES]
        return carry

    lax.fori_loop(0, n_sub, sub_body, 0)

    def out_body(g, carry):
        b0 = pl.multiple_of(g * gsz, gsz)
        r0 = pl.multiple_of(g * ROW_GROUP, ROW_GROUP)
        ya = ya_s[pl.ds(r0, ROW_GROUP), :].astype(BF16)
        gate_b = _silu(gb_s[pl.ds(r0, ROW_GROUP), :])
        yb0 = (yb_s[0, pl.ds(r0, ROW_GROUP), :] * gate_b[:, 0:LANES]).astype(BF16)
        yb1 = (yb_s[1, pl.ds(r0, ROW_GROUP), :] * gate_b[:, LANES:2 * LANES]).astype(BF16)
        yc = yc_s[pl.ds(r0, ROW_GROUP), :].astype(BF16)
        for c in range(0, D_MODEL, MXU_N):
            cc = slice(c, c + MXU_N)
            y = (jnp.dot(ya, wout_ref[0:D_ATTN, cc], preferred_element_type=F32)
                 + jnp.dot(yb0, wout_ref[D_ATTN:D_ATTN + LANES, cc], preferred_element_type=F32)
                 + jnp.dot(yb1, wout_ref[D_ATTN + LANES:D_ATTN + D_SSM, cc], preferred_element_type=F32)
                 + jnp.dot(yc, wout_ref[D_ATTN + D_SSM:D_MODEL, cc], preferred_element_type=F32))
            h1 = h_ref[pl.ds(b0, gsz), :, cc].reshape(ROW_GROUP, MXU_N) + y
            h1_s[:, cc] = h1
            h1b_s[:, cc] = h1.astype(BF16)
        pb = p_ref[pl.ds(b0, gsz)].reshape(ROW_GROUP, PLE_DIM).astype(BF16)
        ssq = jnp.zeros((ROW_GROUP, 1), F32)
        for c in range(0, D_MODEL, MXU_N):
            cc = slice(c, c + MXU_N)
            gate = jax.nn.sigmoid(jnp.dot(h1b_s[...], wgate_ref[:, cc], preferred_element_type=F32))
            pp = jnp.dot(pb, wpp_ref[:, cc], preferred_element_type=F32)
            h2 = h1_s[:, cc] + gate * pp
            if final_norm:
                ssq = ssq + jnp.sum(h2 * h2, axis=-1, keepdims=True)
                h1_s[:, cc] = h2
            else:
                hout_ref[pl.ds(b0, gsz), :, cc] = h2.reshape(gsz, tcp, MXU_N)
        if final_norm:
            scale = lax.rsqrt(ssq * (1.0 / D_MODEL) + EPS)
            hout_ref[pl.ds(b0, gsz)] = (h1_s[...] * scale * fg_ref[...]).reshape(gsz, tcp, D_MODEL)
        return carry

    lax.fori_loop(0, n_groups, out_body, 0)

    @pl.when(n == n_chunks - 1)
    def _finish():
        sreo_ref[...] = sre[...]
        simo_ref[...] = sim[...]
        ro_ref[...] = rst[...]


def _rope_tables(pos0, t_pad):
    pos = jnp.arange(t_pad, dtype=F32) + pos0
    lane = np.arange(LANES) % HEAD_DIM

    def tables(theta, rot_dim):
        half = rot_dim // 2
        inv = 1.0 / jnp.power(theta, jnp.arange(half, dtype=F32) / half)
        ang = pos[:, None] * inv[None, :]
        cos = jnp.cos(ang)
        sin = jnp.sin(ang)
        idx = np.where(lane < rot_dim, lane % half, 0)
        in_rot = jnp.asarray(lane < rot_dim)
        is_hi = jnp.asarray((lane >= half) & (lane < rot_dim))
        is_lo = jnp.asarray(lane < half)
        cos_l = jnp.where(in_rot[None, :], cos[:, idx], 1.0)
        sin_l = sin[:, idx]
        return (cos_l, jnp.where(is_hi[None, :], sin_l, 0.0), jnp.where(is_lo[None, :], -sin_l, 0.0))

    return jnp.stack(tables(ROPE_THETA, ROT_DIM) + tables(RET_THETA, HEAD_DIM))


def _retention_tables(tc, tcp):
    gammas = 1.0 - jnp.power(2.0, -5.0 - jnp.arange(N_RET_HEADS, dtype=F32))
    log_g = jnp.log(gammas)
    i = jnp.arange(tcp, dtype=F32)
    rel = i[:, None] - i[None, :]
    dmask = jnp.where(rel[None] >= 0, jnp.exp(log_g[:, None, None] * jnp.maximum(rel, 0.0)[None]), 0.0)
    kdec = jnp.exp(log_g[None, :] * (tc - 1.0 - i)[:, None])
    qdec = jnp.exp(log_g[None, :] * (i + 1.0)[:, None])
    gch = jnp.broadcast_to(jnp.exp(log_g * tc)[None, :], (tcp, N_RET_HEADS))
    rdec = jnp.stack([jnp.repeat(t, HEAD_DIM, axis=1) for t in (kdec, qdec, gch)])
    return dmask, rdec


def _layer_call(cfg, batch, layer, h, p_all, rope, dmask, rdec, kwin, vwin, sre0, sim0, r0, weights):
    bblk, tc, tcp, n_chunks, pos0, final_norm = cfg
    rows = bblk * tcp
    rows_sb = min(tc, ROW_GROUP // bblk) * bblk
    nb = batch // bblk
    t_pad = n_chunks * tcp
    const = pl.Buffered(1)

    def wspec(shape):
        nd = len(shape)
        return pl.BlockSpec((None,) + tuple(shape[1:]), lambda bb, n: (layer,) + (0,) * (nd - 1),
                            pipeline_mode=const)

    def sspec(shape, lead_layer):
        nd = len(shape)
        if lead_layer:
            return pl.BlockSpec((None, bblk) + tuple(shape[2:]), lambda bb, n: (layer, bb) + (0,) * (nd - 2))
        return pl.BlockSpec((bblk,) + tuple(shape[1:]), lambda bb, n: (bb,) + (0,) * (nd - 1))

    (sink, ng, w_in, lam, bbd, wc, d_row, wglu, rg, w_out, w_pp, w_gate, fg) = weights
    in_specs = [
        pl.BlockSpec((bblk, tcp, D_MODEL), lambda bb, n: (bb, n, 0)),
        pl.BlockSpec((None, bblk, tcp, PLE_DIM), lambda bb, n: (layer, bb, n, 0)),
        pl.BlockSpec((6, tcp, LANES), lambda bb, n: (0, n, 0)),
        pl.BlockSpec(dmask.shape, lambda bb, n: (0, 0, 0), pipeline_mode=const),
        pl.BlockSpec(rdec.shape, lambda bb, n: (0, 0, 0), pipeline_mode=const),
        sspec(kwin.shape, kwin.ndim == 4), sspec(vwin.shape, vwin.ndim == 4),
        sspec(sre0.shape, sre0.ndim == 3), sspec(sim0.shape, sim0.ndim == 3),
        sspec(r0.shape, r0.ndim == 5),
        pl.BlockSpec(memory_space=pltpu.SMEM),
        wspec(ng.shape), wspec(w_in.shape), wspec(lam.shape), wspec(bbd.shape), wspec(wc.shape),
        wspec(d_row.shape), wspec(wglu.shape), wspec(rg.shape), wspec(w_out.shape), wspec(w_pp.shape),
        wspec(w_gate.shape),
        pl.BlockSpec(fg.shape, lambda bb, n: (0, 0), pipeline_mode=const),
    ]
    out_shape = (
        jax.ShapeDtypeStruct((batch, t_pad, D_MODEL), F32),
        jax.ShapeDtypeStruct((batch, tcp, LANES), F32),
        jax.ShapeDtypeStruct((batch, tcp, LANES), F32),
        jax.ShapeDtypeStruct((batch, D_STATE), F32),
        jax.ShapeDtypeStruct((batch, D_STATE), F32),
        jax.ShapeDtypeStruct((batch, 2, LANES, LANES), F32),
    )
    out_specs = (
        pl.BlockSpec((bblk, tcp, D_MODEL), lambda bb, n: (bb, n, 0)),
        pl.BlockSpec((bblk, tcp, LANES), lambda bb, n: (bb, 0, 0)),
        pl.BlockSpec((bblk, tcp, LANES), lambda bb, n: (bb, 0, 0)),
        pl.BlockSpec((bblk, D_STATE), lambda bb, n: (bb, 0)),
        pl.BlockSpec((bblk, D_STATE), lambda bb, n: (bb, 0)),
        pl.BlockSpec((bblk, 2, LANES, LANES), lambda bb, n: (bb, 0, 0, 0)),
    )
    scratch = [
        pltpu.VMEM((bblk, WINDOW, LANES), F32),
        pltpu.VMEM((bblk, WINDOW, LANES), F32),
        pltpu.VMEM((bblk, D_STATE), F32),
        pltpu.VMEM((bblk, D_STATE), F32),
        pltpu.VMEM((bblk, 2, LANES, LANES), F32),
        pltpu.VMEM((D_SSM, 2 * D_STATE), BF16),
        pltpu.VMEM((2, D_STATE), F32),
        pltpu.VMEM((ROW_GROUP, D_MODEL), BF16),
        pltpu.VMEM((ROW_GROUP, COL_B - COL_A), F32),
        pltpu.VMEM((ROW_GROUP, D_IN - COL_C), F32),
        pltpu.VMEM((2, rows, LANES), F32),
        pltpu.VMEM((rows, D_SSM), F32),
        pltpu.VMEM((rows, D_ATTN), F32),
        pltpu.VMEM((rows, D_RET), F32),
        pltpu.VMEM((2, rows, LANES), F32),
        pltpu.VMEM((rows_sb, D_SSM), F32),
        pltpu.VMEM((rows_sb, 2 * D_STATE), F32),
        pltpu.VMEM((ROW_GROUP, D_MODEL), F32),
        pltpu.VMEM((ROW_GROUP, D_MODEL), BF16),
    ]
    return pl.pallas_call(
        functools.partial(_layer_kernel, cfg),
        grid=(nb, n_chunks),
        in_specs=in_specs,
        out_specs=out_specs,
        out_shape=out_shape,
        scratch_shapes=scratch,
        compiler_params=pltpu.CompilerParams(
            dimension_semantics=("arbitrary", "arbitrary"),
            vmem_limit_bytes=VMEM_LIMIT),
        name="layer_%s_%d" % ("prompt" if n_chunks > 1 else "sample", layer),
    )(h, p_all, rope, dmask, rdec, kwin, vwin, sre0, sim0, r0,
      sink, ng, w_in, lam, bbd, wc, d_row, wglu, rg, w_out, w_pp, w_gate, fg)


def _blockdiag_pairs(r):
    b = r.shape[0]
    r5 = r.reshape(b, 2, 2, HEAD_DIM, HEAD_DIM)
    eye = jnp.eye(2, dtype=r.dtype)
    return (r5[:, :, :, :, None, :] * eye[None, None, :, None, :, None]).reshape(b, 2, LANES, LANES)


def _diag_blocks(rbd):
    b = rbd.shape[0]
    r6 = rbd.reshape(b, 2, 2, HEAD_DIM, 2, HEAD_DIM)
    return jnp.stack([r6[:, :, 0, :, 0, :], r6[:, :, 1, :, 1, :]], axis=2).reshape(
        b, N_RET_HEADS, HEAD_DIM, HEAD_DIM)


def kernel(x_prompt, x_sample, p_prompt, p_sample, cache_win_k, cache_win_v, state_ssm_re, state_ssm_im,
           state_ret, norm_g, w_in, attn_sink, ssm_lam_re, ssm_lam_im, ssm_log_dt, ssm_b_re, ssm_b_im,
           ssm_c_re, ssm_c_im, ssm_d, w_glu, ret_norm_g, w_out, w_ple_proj, w_ple_gate, final_norm_g):
    depth = w_in.shape[0]
    bp, seq = x_prompt.shape[0], x_prompt.shape[1]
    bs, dec_seq = x_sample.shape[0], x_sample.shape[1]
    past_len = 8192

    eye_g = jnp.eye(N_SSM_GROUPS, dtype=F32)
    bbd = jnp.stack([
        (b.transpose(0, 1, 3, 2)[:, :, :, None, :] * eye_g[None, :, None, :, None]).reshape(depth, D_SSM, D_STATE)
        for b in (ssm_b_re, ssm_b_im)], axis=1)
    wc = jnp.stack([
        (c.transpose(0, 1, 3, 2)[:, :, :, None, :] * eye_g[None, :, None, :, None]).reshape(depth, D_STATE, D_SSM)
        for c in (ssm_c_re, ssm_c_im)], axis=1).astype(BF16)
    lam = jnp.stack([ssm_lam_re.reshape(depth, D_STATE), ssm_lam_im.reshape(depth, D_STATE),
                     jnp.repeat(ssm_log_dt, SSM_STATE, axis=-1)], axis=1)
    weights = (
        None,
        norm_g.reshape(depth, 1, D_MODEL), w_in.astype(BF16), lam, bbd, wc,
        ssm_d.reshape(depth, 1, D_SSM), w_glu.astype(BF16), ret_norm_g.reshape(depth, 1, D_RET),
        w_out.astype(BF16), w_ple_proj.astype(BF16), w_ple_gate.astype(BF16),
        final_norm_g.reshape(1, D_MODEL),
    )

    tc_p = WINDOW
    cfg_p = lambda last: (bp, tc_p, tc_p, seq // tc_p, 0, last)
    rope_p = _rope_tables(0.0, seq)
    dmask_p, rdec_p = _retention_tables(tc_p, tc_p)
    zk = jnp.zeros((bp, WINDOW, LANES), F32)
    zs = jnp.zeros((bp, D_STATE), F32)
    zr = jnp.zeros((bp, 2, LANES, LANES), F32)

    tcp_s = 16
    bblk_s = ROW_GROUP // tcp_s
    cfg_s = lambda last: (bblk_s, dec_seq, tcp_s, 1, past_len, last)
    rope_s = _rope_tables(float(past_len), tcp_s)
    dmask_s, rdec_s = _retention_tables(dec_seq, tcp_s)
    pad_t = ((0, 0), (0, tcp_s - dec_seq), (0, 0))
    hs = jnp.pad(x_sample, pad_t)
    ps = jnp.pad(p_sample, ((0, 0),) + pad_t)
    kwin_s = cache_win_k.reshape(depth, bs, WINDOW, LANES)
    vwin_s = cache_win_v.reshape(depth, bs, WINDOW, LANES)
    sre_s = state_ssm_re.reshape(depth, bs, D_STATE)
    sim_s = state_ssm_im.reshape(depth, bs, D_STATE)
    r_s = _blockdiag_pairs(state_ret.reshape(depth * bs, N_RET_HEADS, HEAD_DIM, HEAD_DIM)).reshape(
        depth, bs, 2, LANES, LANES)

    hp = x_prompt
    outs_p, outs_s = [], []
    for i in range(depth):
        last = i == depth - 1
        w_i = (attn_sink[i].reshape(1, N_Q_HEADS),) + weights[1:]
        hp, kp, vp, srp, sip, rp = _layer_call(cfg_p(last), bp, i, hp, p_prompt, rope_p, dmask_p, rdec_p,
                                               zk, zk, zs, zs, zr, w_i)
        hs, kc, vc, srs, sis, rs = _layer_call(cfg_s(last), bs, i, hs, ps, rope_s, dmask_s, rdec_s,
                                               kwin_s, vwin_s, sre_s, sim_s, r_s, w_i)
        outs_p.append((kp, vp, srp, sip, _diag_blocks(rp)))
        ks = jnp.concatenate([kwin_s[i][:, dec_seq:], kc[:, :dec_seq]], axis=1)
        vs = jnp.concatenate([vwin_s[i][:, dec_seq:], vc[:, :dec_seq]], axis=1)
        outs_s.append((ks, vs, srs, sis, _diag_blocks(rs)))

    def pack(outs, b):
        k, v, sr, si, r = (jnp.stack(t) for t in zip(*outs))
        return (k.reshape(depth, b, WINDOW, N_KV_HEADS, HEAD_DIM), v.reshape(depth, b, WINDOW, N_KV_HEADS, HEAD_DIM),
                sr.reshape(depth, b, N_SSM_GROUPS, SSM_STATE), si.reshape(depth, b, N_SSM_GROUPS, SSM_STATE), r)

    return (hp, hs[:, :dec_seq]) + pack(outs_p, bp) + pack(outs_s, bs)
```

```python
import functools
import math

import numpy as np
import jax
import jax.numpy as jnp
from jax import lax
from jax.experimental import pallas as pl
from jax.experimental.pallas import tpu as pltpu

F32 = jnp.float32
BF16 = jnp.bfloat16

D_MODEL = 1024
HEAD_DIM = 64
D_ATTN = 512
N_Q_HEADS = 8
N_KV_HEADS = 2
D_KV = 128
WINDOW = 128
ROPE_THETA = 500000.0
ROT_DIM = 16
D_SSM = 256
SSM_GROUP = 16
N_SSM_GROUPS = 16
SSM_STATE = 64
D_STATE = N_SSM_GROUPS * SSM_STATE
D_RET = 256
N_RET_HEADS = 4
RET_THETA = 10000.0
PLE_DIM = 256
EPS = 1e-6
D_IN = 2816
COL_A = 0
COL_B = 1280
COL_C = 1792
LANES = 128
MXU_N = 256
ROW_GROUP = 256
VMEM_LIMIT = 58 * 1024 * 1024


def _mm(a, b):
    return jnp.dot(a.astype(BF16), b.astype(BF16), preferred_element_type=F32)


def _mm_nt(a, b):
    return lax.dot_general(a.astype(BF16), b.astype(BF16), (((1,), (1,)), ((), ())),
                           preferred_element_type=F32)


def _silu(x):
    return x * jax.nn.sigmoid(x)


def _layer_kernel(cfg,
                  h_ref, p_ref, rope_ref, dmask_ref, rdec_ref,
                  kwin_ref, vwin_ref, sre0_ref, sim0_ref, r0_ref,
                  sink_ref, ng_ref, win_ref, lam_ref, bbd_ref, wc_ref, d_ref, wglu_ref, rg_ref,
                  wout_ref, wpp_ref, wgate_ref, fg_ref,
                  hout_ref, kout_ref, vout_ref, sreo_ref, simo_ref, ro_ref,
                  kbuf, vbuf, sre, sim, rst, wb_s, a_s, hn_s, pa_s, pc_s, u_s, gb_s,
                  ya_s, yc_s, yb_s, ut_s, bu_s, h1_s, h1b_s):
    bblk, tc, tcp, n_chunks, pos0, final_norm = cfg
    gsz = ROW_GROUP // tcp
    n_groups = bblk // gsz
    sb = min(tc, ROW_GROUP // bblk)
    n_sub = tc // sb
    rows_sb = sb * bblk
    n = pl.program_id(1)

    @pl.when(n == 0)
    def _init():
        kbuf[:, 0:WINDOW] = kwin_ref[...]
        vbuf[:, 0:WINDOW] = vwin_ref[...]
        if tcp < WINDOW:
            kbuf[:, WINDOW:2 * WINDOW] = jnp.zeros((bblk, WINDOW, LANES), F32)
            vbuf[:, WINDOW:2 * WINDOW] = jnp.zeros((bblk, WINDOW, LANES), F32)
        sre[...] = sre0_ref[...]
        sim[...] = sim0_ref[...]
        r_in = r0_ref[...].reshape(bblk, N_RET_HEADS // 2, LANES, HEAD_DIM)
        row_d = lax.broadcasted_iota(jnp.int32, (LANES, LANES), 0) < HEAD_DIM
        col_d = lax.broadcasted_iota(jnp.int32, (LANES, LANES), 1) < HEAD_DIM
        rst[...] = jnp.where(row_d == col_d, jnp.concatenate([r_in, r_in], axis=-1), 0.0)
        lr = lam_ref[0:1, :]
        li = lam_ref[1:2, :]
        dt = jnp.exp(lam_ref[2:3, :])
        mag = jnp.exp(lr * dt)
        a_re = mag * jnp.cos(li * dt)
        a_im = mag * jnp.sin(li * dt)
        nr = a_re - 1.0
        ni = a_im
        den = lr * lr + li * li
        coef_re = (nr * lr + ni * li) / den
        coef_im = (ni * lr - nr * li) / den
        a_s[0:1, :] = a_re
        a_s[1:2, :] = a_im
        br = bbd_ref[0]
        bi = bbd_ref[1]
        wb_s[:, 0:D_STATE] = (coef_re * br - coef_im * bi).astype(BF16)
        wb_s[:, D_STATE:2 * D_STATE] = (coef_re * bi + coef_im * br).astype(BF16)
        if tc < tcp:
            yb_s[...] = jnp.zeros_like(yb_s)

    def lane_lo(rows):
        return lax.broadcasted_iota(jnp.int32, (rows, LANES), 1) < HEAD_DIM

    lo_q = lane_lo(tcp)
    lo_k = lane_lo(2 * WINDOW)
    row_w = lax.broadcasted_iota(jnp.int32, (WINDOW, LANES), 0)
    blockdiag = (row_w < HEAD_DIM) == lane_lo(WINDOW)
    headsum = jnp.where(blockdiag, 1.0 / HEAD_DIM, 0.0).astype(BF16)
    headsum2 = jnp.concatenate([headsum, headsum], axis=0)
    ones_k = jnp.ones((2 * WINDOW, LANES), BF16)

    cos_a, sa_hi, sa_lo = rope_ref[0], rope_ref[1], rope_ref[2]
    cos_r, sr_hi, sr_lo = rope_ref[3], rope_ref[4], rope_ref[5]

    def rope_a(x):
        return (x * cos_a + pltpu.roll(x, ROT_DIM // 2, 1) * sa_hi
                + pltpu.roll(x, LANES - ROT_DIM // 2, 1) * sa_lo)

    def rope_r(x):
        return (x * cos_r + pltpu.roll(x, HEAD_DIM // 2, 1) * sr_hi
                + pltpu.roll(x, LANES - HEAD_DIM // 2, 1) * sr_lo)

    def kv_variants(x):
        xr = pltpu.roll(x, HEAD_DIM, 1)
        return (jnp.where(lo_k, x, 0.0), jnp.where(lo_k, 0.0, xr),
                jnp.where(lo_k, xr, 0.0), jnp.where(lo_k, 0.0, x))

    row2 = lax.broadcasted_iota(jnp.int32, (2 * tcp, 2 * WINDOW), 0)
    qi = jnp.where(row2 < tcp, row2, row2 - tcp)
    kl = lax.broadcasted_iota(jnp.int32, (2 * tcp, 2 * WINDOW), 1)
    lmin = WINDOW - pos0 - n * tc
    valid = (kl > qi) & (kl <= qi + WINDOW) & (kl >= lmin)
    first_head = lax.broadcasted_iota(jnp.int32, (2 * tcp, 1), 0) < tcp

    def group_body(g, carry):
        b0 = pl.multiple_of(g * gsz, gsz)
        r0 = pl.multiple_of(g * ROW_GROUP, ROW_GROUP)
        x = h_ref[pl.ds(b0, gsz)].reshape(ROW_GROUP, D_MODEL)
        ms = jnp.mean(x * x, axis=-1, keepdims=True)
        hn_s[...] = (x * lax.rsqrt(ms + EPS) * ng_ref[...]).astype(BF16)
        for c in range(0, COL_B - COL_A, MXU_N):
            pa_s[:, c:c + MXU_N] = jnp.dot(hn_s[...], win_ref[:, COL_A + c:COL_A + c + MXU_N],
                                           preferred_element_type=F32)
        for c in range(0, D_IN - COL_C, MXU_N):
            pc_s[:, c:c + MXU_N] = jnp.dot(hn_s[...], win_ref[:, COL_C + c:COL_C + c + MXU_N],
                                           preferred_element_type=F32)
        ub = jnp.dot(hn_s[...], win_ref[:, COL_B:COL_B + D_SSM], preferred_element_type=F32)
        u_s[0, pl.ds(r0, ROW_GROUP), :] = ub[:, 0:LANES]
        u_s[1, pl.ds(r0, ROW_GROUP), :] = ub[:, LANES:2 * LANES]
        gb_s[pl.ds(r0, ROW_GROUP), :] = jnp.dot(hn_s[...], win_ref[:, COL_B + D_SSM:COL_C],
                                                preferred_element_type=F32)

        def batch_body(j, carry2):
            b = b0 + j
            rj = pl.multiple_of(j * tcp, tcp)
            rb = pl.multiple_of(b * tcp, tcp)
            k_cur = rope_a(pa_s[pl.ds(rj, tcp), D_ATTN:D_ATTN + D_KV])
            v_cur = pa_s[pl.ds(rj, tcp), D_ATTN + D_KV:D_ATTN + 2 * D_KV]
            if tcp == WINDOW:
                kout_ref[b] = k_cur
                vout_ref[b] = v_cur
            else:
                keep = lax.broadcasted_iota(jnp.int32, (tcp, LANES), 0) < tcp - tc
                for buf, cur, out in ((kbuf, k_cur, kout_ref), (vbuf, v_cur, vout_ref)):
                    shifted = pltpu.roll(buf[b, 0:WINDOW], WINDOW - tc, 0)
                    out[b, 0:WINDOW - tcp] = shifted[0:WINDOW - tcp]
                    out[b, WINDOW - tcp:WINDOW] = jnp.where(keep, shifted[WINDOW - tcp:WINDOW],
                                                            pltpu.roll(cur, tcp - tc, 0))
            kbuf[b, WINDOW:WINDOW + tcp] = k_cur
            vbuf[b, WINDOW:WINDOW + tcp] = v_cur
            k_var = kv_variants(kbuf[b])
            v_var = kv_variants(vbuf[b])
            for kv in range(N_KV_HEADS):
                qs = jnp.concatenate(
                    [rope_a(pa_s[pl.ds(rj, tcp), (2 * kv + t) * LANES:(2 * kv + t + 1) * LANES]) for t in range(2)],
                    axis=0).astype(BF16)
                o = jnp.zeros((2 * tcp, LANES), F32)
                for e in range(2):
                    vi = 2 * kv + e
                    s = _mm_nt(qs, k_var[vi]) * (HEAD_DIM ** -0.5)
                    s = jnp.where(valid, s, -jnp.inf)
                    sk = jnp.where(first_head, sink_ref[0, 4 * kv + e], sink_ref[0, 4 * kv + 2 + e])
                    m = jnp.maximum(jnp.max(s, axis=-1, keepdims=True), sk)
                    ex = jnp.exp(s - m)
                    r = _mm(ex, jnp.concatenate([v_var[vi].astype(BF16), ones_k], axis=1))
                    den = r[:, LANES:2 * LANES] + jnp.exp(sk - m)
                    o = o + r[:, 0:LANES] / den
                for t in range(2):
                    pj = 2 * kv + t
                    ga = pa_s[pl.ds(rj, tcp),
                              D_ATTN + 2 * D_KV + pj * LANES:D_ATTN + 2 * D_KV + (pj + 1) * LANES]
                    ya_s[pl.ds(rb, tcp), pj * LANES:(pj + 1) * LANES] = o[t * tcp:(t + 1) * tcp] * _silu(ga)
            if n_chunks > 1:
                kbuf[b, 0:WINDOW] = k_cur
                vbuf[b, 0:WINDOW] = v_cur
            for pr in range(N_RET_HEADS // 2):
                cs = slice(pr * LANES, (pr + 1) * LANES)
                q = rope_r(pc_s[pl.ds(rj, tcp), pr * LANES:(pr + 1) * LANES])
                k = rope_r(pc_s[pl.ds(rj, tcp), D_RET + pr * LANES:D_RET + (pr + 1) * LANES]) * (HEAD_DIM ** -0.5)
                v = pc_s[pl.ds(rj, tcp), 2 * D_RET + pr * LANES:2 * D_RET + (pr + 1) * LANES]
                gr = pc_s[pl.ds(rj, tcp), 3 * D_RET + pr * LANES:3 * D_RET + (pr + 1) * LANES]
                r_old = rst[b, pr]
                k_heads = jnp.concatenate([jnp.where(lo_q, k, 0.0), jnp.where(lo_q, 0.0, k)], axis=0)
                v_heads = jnp.concatenate([jnp.where(lo_q, v, 0.0), jnp.where(lo_q, 0.0, v)], axis=0)
                sc = _mm_nt(q, k_heads) * dmask_ref[pr]
                o = _mm(q, r_old) * rdec_ref[1, :, cs] + _mm(sc, v_heads)
                kd = k * rdec_ref[0, :, cs]
                if tcp < WINDOW:
                    pad = jnp.zeros((WINDOW - tcp, LANES), F32)
                    d_new = _mm(jnp.concatenate([kd, pad], axis=0).T, jnp.concatenate([v, pad], axis=0))
                else:
                    d_new = _mm(kd.T, v)
                rst[b, pr] = r_old * rdec_ref[2, 0:1, cs] + jnp.where(blockdiag, d_new, 0.0)
                o2 = o * o
                o2_hi = o2.astype(BF16)
                o2_lo = (o2 - o2_hi.astype(F32)).astype(BF16)
                msq = jnp.dot(jnp.concatenate([o2_hi, o2_lo], axis=1), headsum2, preferred_element_type=F32)
                yc_s[pl.ds(rb, tcp), pr * LANES:(pr + 1) * LANES] = (
                    o * lax.rsqrt(msq + EPS) * rg_ref[:, cs] * _silu(gr))
            return carry2

        lax.fori_loop(0, gsz, batch_body, 0, unroll=min(gsz, 8))
        return carry

    lax.fori_loop(0, n_groups, group_body, 0)

    a_re_b = jnp.broadcast_to(a_s[0:1, :], (bblk, D_STATE))
    a_im_b = jnp.broadcast_to(a_s[1:2, :], (bblk, D_STATE))

    def sub_body(si, carry):
        t0 = si * sb
        for t in range(sb):
            for s in range(2):
                ut_s[t * bblk:(t + 1) * bblk, s * LANES:(s + 1) * LANES] = (
                    u_s[s, pl.ds(t0 + t, bblk, stride=tcp), :])
        for c in range(0, 2 * D_STATE, MXU_N):
            bu_s[0:rows_sb, c:c + MXU_N] = _mm(ut_s[0:rows_sb, :], wb_s[:, c:c + MXU_N])

        def step(t, xs):
            xr, xi = xs
            rt = pl.multiple_of(t * bblk, bblk)
            b_re = bu_s[pl.ds(rt, bblk), 0:D_STATE]
            b_im = bu_s[pl.ds(rt, bblk), D_STATE:2 * D_STATE]
            nxr = a_re_b * xr - a_im_b * xi + b_re
            nxi = a_re_b * xi + a_im_b * xr + b_im
            bu_s[pl.ds(rt, bblk), 0:D_STATE] = nxr
            bu_s[pl.ds(rt, bblk), D_STATE:2 * D_STATE] = nxi
            return nxr, nxi

        xr, xi = lax.fori_loop(0, sb, step, (sre[...], sim[...]))
        sre[...] = xr
        sim[...] = xi
        y = (_mm(bu_s[0:rows_sb, 0:D_STATE], wc_ref[0]) - _mm(bu_s[0:rows_sb, D_STATE:2 * D_STATE], wc_ref[1])
             + d_ref[...] * ut_s[0:rows_sb, :])
        y = jax.nn.gelu(y)
        y = y * jax.nn.sigmoid(_mm(y, wglu_ref[...]))
        for t in range(sb):
            for s in range(2):
                yb_s[s, pl.ds(t0 + t, bblk, stride=tcp), :] = y[t * bblk:(t + 1) * bblk, s * LANES:(s + 1) * LANES]
        return carry

    lax.fori_loop(0, n_sub, sub_body, 0)

    def out_body(g, carry):
        b0 = pl.multiple_of(g * gsz, gsz)
        r0 = pl.multiple_of(g * ROW_GROUP, ROW_GROUP)
        ya = ya_s[pl.ds(r0, ROW_GROUP), :].astype(BF16)
        gate_b = _silu(gb_s[pl.ds(r0, ROW_GROUP), :])
        yb0 = (yb_s[0, pl.ds(r0, ROW_GROUP), :] * gate_b[:, 0:LANES]).astype(BF16)
        yb1 = (yb_s[1, pl.ds(r0, ROW_GROUP), :] * gate_b[:, LANES:2 * LANES]).astype(BF16)
        yc = yc_s[pl.ds(r0, ROW_GROUP), :].astype(BF16)
        for c in range(0, D_MODEL, MXU_N):
            cc = slice(c, c + MXU_N)
            y = (jnp.dot(ya, wout_ref[0:D_ATTN, cc], preferred_element_type=F32)
                 + jnp.dot(yb0, wout_ref[D_ATTN:D_ATTN + LANES, cc], preferred_element_type=F32)
                 + jnp.dot(yb1, wout_ref[D_ATTN + LANES:D_ATTN + D_SSM, cc], preferred_element_type=F32)
                 + jnp.dot(yc, wout_ref[D_ATTN + D_SSM:D_MODEL, cc], preferred_element_type=F32))
            h1 = h_ref[pl.ds(b0, gsz), :, cc].reshape(ROW_GROUP, MXU_N) + y
            h1_s[:, cc] = h1
            h1b_s[:, cc] = h1.astype(BF16)
        pb = p_ref[pl.ds(b0, gsz)].reshape(ROW_GROUP, PLE_DIM).astype(BF16)
        ssq = jnp.zeros((ROW_GROUP, 1), F32)
        for c in range(0, D_MODEL, MXU_N):
            cc = slice(c, c + MXU_N)
            gate = jax.nn.sigmoid(jnp.dot(h1b_s[...], wgate_ref[:, cc], preferred_element_type=F32))
            pp = jnp.dot(pb, wpp_ref[:, cc], preferred_element_type=F32)
            h2 = h1_s[:, cc] + gate * pp
            if final_norm:
                ssq = ssq + jnp.sum(h2 * h2, axis=-1, keepdims=True)
                h1_s[:, cc] = h2
            else:
                hout_ref[pl.ds(b0, gsz), :, cc] = h2.reshape(gsz, tcp, MXU_N)
        if final_norm:
            scale = lax.rsqrt(ssq * (1.0 / D_MODEL) + EPS)
            hout_ref[pl.ds(b0, gsz)] = (h1_s[...] * scale * fg_ref[...]).reshape(gsz, tcp, D_MODEL)
        return carry

    lax.fori_loop(0, n_groups, out_body, 0)

    @pl.when(n == n_chunks - 1)
    def _finish():
        sreo_ref[...] = sre[...]
        simo_ref[...] = sim[...]
        r_fin = rst[...]
        ro_ref[...] = (r_fin[..., 0:HEAD_DIM] + r_fin[..., HEAD_DIM:LANES]).reshape(
            bblk, N_RET_HEADS * HEAD_DIM, HEAD_DIM)


def _rope_tables(pos0, t_pad):
    pos = jnp.arange(t_pad, dtype=F32) + pos0
    lane = np.arange(LANES) % HEAD_DIM

    def tables(theta, rot_dim):
        half = rot_dim // 2
        inv = 1.0 / jnp.power(theta, jnp.arange(half, dtype=F32) / half)
        ang = pos[:, None] * inv[None, :]
        cos = jnp.cos(ang)
        sin = jnp.sin(ang)
        idx = np.where(lane < rot_dim, lane % half, 0)
        in_rot = jnp.asarray(lane < rot_dim)
        is_hi = jnp.asarray((lane >= half) & (lane < rot_dim))
        is_lo = jnp.asarray(lane < half)
        cos_l = jnp.where(in_rot[None, :], cos[:, idx], 1.0)
        sin_l = sin[:, idx]
        return (cos_l, jnp.where(is_hi[None, :], sin_l, 0.0), jnp.where(is_lo[None, :], -sin_l, 0.0))

    return jnp.stack(tables(ROPE_THETA, ROT_DIM) + tables(RET_THETA, HEAD_DIM))


def _retention_tables(tc, tcp):
    gammas = 1.0 - jnp.power(2.0, -5.0 - jnp.arange(N_RET_HEADS, dtype=F32))
    log_g = jnp.log(gammas)
    i = jnp.arange(tcp, dtype=F32)
    rel = i[:, None] - i[None, :]
    dmask = jnp.where(rel[None] >= 0, jnp.exp(log_g[:, None, None] * jnp.maximum(rel, 0.0)[None]), 0.0)
    kdec = jnp.exp(log_g[None, :] * (tc - 1.0 - i)[:, None])
    qdec = jnp.exp(log_g[None, :] * (i + 1.0)[:, None])
    gch = jnp.broadcast_to(jnp.exp(log_g * tc)[None, :], (tcp, N_RET_HEADS))
    rdec = jnp.stack([jnp.repeat(t, HEAD_DIM, axis=1) for t in (kdec, qdec, gch)])
    dmask = dmask.reshape(N_RET_HEADS // 2, 2, tcp, tcp).transpose(0, 2, 1, 3).reshape(
        N_RET_HEADS // 2, tcp, 2 * tcp)
    return dmask, rdec


def _layer_call(cfg, batch, layer, h, p_all, rope, dmask, rdec, kwin, vwin, sre0, sim0, r0, weights):
    bblk, tc, tcp, n_chunks, pos0, final_norm = cfg
    rows = bblk * tcp
    rows_sb = min(tc, ROW_GROUP // bblk) * bblk
    nb = batch // bblk
    t_pad = n_chunks * tcp
    const = pl.Buffered(1)

    def wspec(shape):
        nd = len(shape)
        return pl.BlockSpec((None,) + tuple(shape[1:]), lambda bb, n: (layer,) + (0,) * (nd - 1),
                            pipeline_mode=const)

    def sspec(shape, lead_layer):
        nd = len(shape)
        if lead_layer:
            return pl.BlockSpec((None, bblk) + tuple(shape[2:]), lambda bb, n: (layer, bb) + (0,) * (nd - 2))
        return pl.BlockSpec((bblk,) + tuple(shape[1:]), lambda bb, n: (bb,) + (0,) * (nd - 1))

    (sink, ng, w_in, lam, bbd, wc, d_row, wglu, rg, w_out, w_pp, w_gate, fg) = weights
    in_specs = [
        pl.BlockSpec((bblk, tcp, D_MODEL), lambda bb, n: (bb, n, 0)),
        pl.BlockSpec((None, bblk, tcp, PLE_DIM), lambda bb, n: (layer, bb, n, 0)),
        pl.BlockSpec((6, tcp, LANES), lambda bb, n: (0, n, 0)),
        pl.BlockSpec(dmask.shape, lambda bb, n: (0, 0, 0), pipeline_mode=const),
        pl.BlockSpec(rdec.shape, lambda bb, n: (0, 0, 0), pipeline_mode=const),
        sspec(kwin.shape, kwin.ndim == 4), sspec(vwin.shape, vwin.ndim == 4),
        sspec(sre0.shape, sre0.ndim == 3), sspec(sim0.shape, sim0.ndim == 3),
        sspec(r0.shape, r0.ndim == 4),
        pl.BlockSpec(memory_space=pltpu.SMEM),
        wspec(ng.shape), wspec(w_in.shape), wspec(lam.shape), wspec(bbd.shape), wspec(wc.shape),
        wspec(d_row.shape), wspec(wglu.shape), wspec(rg.shape), wspec(w_out.shape), wspec(w_pp.shape),
        wspec(w_gate.shape),
        pl.BlockSpec(fg.shape, lambda bb, n: (0, 0), pipeline_mode=const),
    ]
    out_shape = (
        jax.ShapeDtypeStruct((batch, t_pad, D_MODEL), F32),
        jax.ShapeDtypeStruct((batch, WINDOW, LANES), F32),
        jax.ShapeDtypeStruct((batch, WINDOW, LANES), F32),
        jax.ShapeDtypeStruct((batch, D_STATE), F32),
        jax.ShapeDtypeStruct((batch, D_STATE), F32),
        jax.ShapeDtypeStruct((batch, N_RET_HEADS * HEAD_DIM, HEAD_DIM), F32),
    )
    out_specs = (
        pl.BlockSpec((bblk, tcp, D_MODEL), lambda bb, n: (bb, n, 0)),
        pl.BlockSpec((bblk, WINDOW, LANES), lambda bb, n: (bb, 0, 0)),
        pl.BlockSpec((bblk, WINDOW, LANES), lambda bb, n: (bb, 0, 0)),
        pl.BlockSpec((bblk, D_STATE), lambda bb, n: (bb, 0)),
        pl.BlockSpec((bblk, D_STATE), lambda bb, n: (bb, 0)),
        pl.BlockSpec((bblk, N_RET_HEADS * HEAD_DIM, HEAD_DIM), lambda bb, n: (bb, 0, 0)),
    )
    scratch = [
        pltpu.VMEM((bblk, 2 * WINDOW, LANES), F32),
        pltpu.VMEM((bblk, 2 * WINDOW, LANES), F32),
        pltpu.VMEM((bblk, D_STATE), F32),
        pltpu.VMEM((bblk, D_STATE), F32),
        pltpu.VMEM((bblk, 2, LANES, LANES), F32),
        pltpu.VMEM((D_SSM, 2 * D_STATE), BF16),
        pltpu.VMEM((2, D_STATE), F32),
        pltpu.VMEM((ROW_GROUP, D_MODEL), BF16),
        pltpu.VMEM((ROW_GROUP, COL_B - COL_A), F32),
        pltpu.VMEM((ROW_GROUP, D_IN - COL_C), F32),
        pltpu.VMEM((2, rows, LANES), F32),
        pltpu.VMEM((rows, D_SSM), F32),
        pltpu.VMEM((rows, D_ATTN), F32),
        pltpu.VMEM((rows, D_RET), F32),
        pltpu.VMEM((2, rows, LANES), F32),
        pltpu.VMEM((rows_sb, D_SSM), F32),
        pltpu.VMEM((rows_sb, 2 * D_STATE), F32),
        pltpu.VMEM((ROW_GROUP, D_MODEL), F32),
        pltpu.VMEM((ROW_GROUP, D_MODEL), BF16),
    ]
    return pl.pallas_call(
        functools.partial(_layer_kernel, cfg),
        grid=(nb, n_chunks),
        in_specs=in_specs,
        out_specs=out_specs,
        out_shape=out_shape,
        scratch_shapes=scratch,
        compiler_params=pltpu.CompilerParams(
            dimension_semantics=("arbitrary", "arbitrary"),
            vmem_limit_bytes=VMEM_LIMIT),
        name="layer_%s_%d" % ("prompt" if n_chunks > 1 else "sample", layer),
    )(h, p_all, rope, dmask, rdec, kwin, vwin, sre0, sim0, r0,
      sink, ng, w_in, lam, bbd, wc, d_row, wglu, rg, w_out, w_pp, w_gate, fg)


def kernel(x_prompt, x_sample, p_prompt, p_sample, cache_win_k, cache_win_v, state_ssm_re, state_ssm_im,
           state_ret, norm_g, w_in, attn_sink, ssm_lam_re, ssm_lam_im, ssm_log_dt, ssm_b_re, ssm_b_im,
           ssm_c_re, ssm_c_im, ssm_d, w_glu, ret_norm_g, w_out, w_ple_proj, w_ple_gate, final_norm_g):
    depth = w_in.shape[0]
    bp, seq = x_prompt.shape[0], x_prompt.shape[1]
    bs, dec_seq = x_sample.shape[0], x_sample.shape[1]
    past_len = 8192

    same_group = jnp.asarray(np.arange(D_SSM)[:, None] // SSM_GROUP == np.arange(D_STATE)[None, :] // SSM_STATE)
    bbd = jnp.stack([
        jnp.where(same_group, jnp.tile(b.transpose(0, 1, 3, 2).reshape(depth, D_SSM, SSM_STATE),
                                       (1, 1, N_SSM_GROUPS)), 0.0)
        for b in (ssm_b_re, ssm_b_im)], axis=1)
    wc = jnp.stack([
        jnp.where(same_group.T, jnp.tile(c.transpose(0, 1, 3, 2).reshape(depth, D_STATE, SSM_GROUP),
                                         (1, 1, N_SSM_GROUPS)), 0.0).astype(BF16)
        for c in (ssm_c_re, ssm_c_im)], axis=1)
    lam = jnp.stack([ssm_lam_re.reshape(depth, D_STATE), ssm_lam_im.reshape(depth, D_STATE),
                     jnp.repeat(ssm_log_dt, SSM_STATE, axis=-1)], axis=1)
    weights = (
        None,
        norm_g.reshape(depth, 1, D_MODEL), w_in.astype(BF16), lam, bbd, wc,
        ssm_d.reshape(depth, 1, D_SSM), w_glu.astype(BF16), ret_norm_g.reshape(depth, 1, D_RET),
        w_out.astype(BF16), w_ple_proj.astype(BF16), w_ple_gate.astype(BF16),
        final_norm_g.reshape(1, D_MODEL),
    )

    tc_p = WINDOW
    cfg_p = lambda last: (bp, tc_p, tc_p, seq // tc_p, 0, last)
    rope_p = _rope_tables(0.0, seq)
    dmask_p, rdec_p = _retention_tables(tc_p, tc_p)
    zk = jnp.zeros((bp, WINDOW, LANES), F32)
    zs = jnp.zeros((bp, D_STATE), F32)
    zr = jnp.zeros((bp, N_RET_HEADS * HEAD_DIM, HEAD_DIM), F32)

    tcp_s = 16
    bblk_s = ROW_GROUP // tcp_s
    cfg_s = lambda last: (bblk_s, dec_seq, tcp_s, 1, past_len, last)
    rope_s = _rope_tables(float(past_len), tcp_s)
    dmask_s, rdec_s = _retention_tables(dec_seq, tcp_s)
    pad_t = ((0, 0), (0, tcp_s - dec_seq), (0, 0))
    hs = jnp.pad(x_sample, pad_t)
    ps = jnp.pad(p_sample, ((0, 0),) + pad_t)
    kwin_s = cache_win_k.reshape(depth, bs, WINDOW, LANES)
    vwin_s = cache_win_v.reshape(depth, bs, WINDOW, LANES)
    sre_s = state_ssm_re.reshape(depth, bs, D_STATE)
    sim_s = state_ssm_im.reshape(depth, bs, D_STATE)
    r_s = state_ret.reshape(depth, bs, N_RET_HEADS * HEAD_DIM, HEAD_DIM)

    hp = x_prompt
    outs_p, outs_s = [], []
    for i in range(depth):
        last = i == depth - 1
        w_i = (attn_sink[i].reshape(1, N_Q_HEADS),) + weights[1:]
        hp, kp, vp, srp, sip, rp = _layer_call(cfg_p(last), bp, i, hp, p_prompt, rope_p, dmask_p, rdec_p,
                                               zk, zk, zs, zs, zr, w_i)
        hs, kc, vc, srs, sis, rs = _layer_call(cfg_s(last), bs, i, hs, ps, rope_s, dmask_s, rdec_s,
                                               kwin_s, vwin_s, sre_s, sim_s, r_s, w_i)
        outs_p.append((kp, vp, srp, sip, rp))
        outs_s.append((kc, vc, srs, sis, rs))

    def pack(outs, b):
        k, v, sr, si, r = (jnp.stack(t) for t in zip(*outs))
        return (k.reshape(depth, b, WINDOW, N_KV_HEADS, HEAD_DIM), v.reshape(depth, b, WINDOW, N_KV_HEADS, HEAD_DIM),
                sr.reshape(depth, b, N_SSM_GROUPS, SSM_STATE), si.reshape(depth, b, N_SSM_GROUPS, SSM_STATE),
                r.reshape(depth, b, N_RET_HEADS, HEAD_DIM, HEAD_DIM))

    return (hp, hs[:, :dec_seq]) + pack(outs_p, bp) + pack(outs_s, bs)
```

```python
import functools
import math

import numpy as np
import jax
import jax.numpy as jnp
from jax import lax
from jax.experimental import pallas as pl
from jax.experimental.pallas import tpu as pltpu

F32 = jnp.float32
BF16 = jnp.bfloat16

D_MODEL = 1024
HEAD_DIM = 64
D_ATTN = 512
N_Q_HEADS = 8
N_KV_HEADS = 2
D_KV = 128
WINDOW = 128
ROPE_THETA = 500000.0
ROT_DIM = 16
D_SSM = 256
SSM_GROUP = 16
N_SSM_GROUPS = 16
SSM_STATE = 64
D_STATE = N_SSM_GROUPS * SSM_STATE
D_RET = 256
N_RET_HEADS = 4
RET_THETA = 10000.0
PLE_DIM = 256
EPS = 1e-6
LOG2E = math.log2(math.e)
D_IN = 2816
COL_A = 0
COL_B = 1280
COL_C = 1792
LANES = 128
MXU_N = 256
ROW_GROUP = 256
VMEM_LIMIT = 58 * 1024 * 1024


def _mm(a, b):
    return jnp.dot(a.astype(BF16), b.astype(BF16), preferred_element_type=F32)


def _mm_nt(a, b):
    return lax.dot_general(a.astype(BF16), b.astype(BF16), (((1,), (1,)), ((), ())),
                           preferred_element_type=F32)


def _silu(x):
    return x * jax.nn.sigmoid(x)


def _layer_kernel(cfg,
                  h_ref, p_ref, rope_ref, dmask_ref, rdec_ref,
                  kwin_ref, vwin_ref, sre0_ref, sim0_ref, r0_ref,
                  sink_ref, ng_ref, win_ref, lam_ref, bbd_ref, wc_ref, d_ref, wglu_ref, rg_ref,
                  wout_ref, wpp_ref, wgate_ref, fg_ref,
                  hout_ref, kout_ref, vout_ref, sreo_ref, simo_ref, ro_ref,
                  kbuf, vbuf, sre, sim, rst, wb_s, a_s, hn_s, pa_s, pc_s, u_s, gb_s,
                  ya_s, yc_s, yb_s, ut_s, bu_s, h1_s, h1b_s, *resident):
    bblk, tc, tcp, n_chunks, pos0, layer0, n_layers, final_norm = cfg
    gsz = ROW_GROUP // tcp
    n_groups = bblk // gsz
    sb = min(tc, ROW_GROUP // bblk)
    assert tc // sb == n_groups
    lyr = pl.program_id(0)
    n = pl.program_id(2)
    if resident:
        h_all, = resident
        hb = pl.multiple_of(pl.program_id(1) * bblk, bblk)

        @pl.when(lyr == 0)
        def _load_h():
            h_all[pl.ds(hb, bblk)] = h_ref[...]

        def h_rows(b0, cols=slice(None)):
            return h_all[pl.ds(hb + b0, gsz), :, cols]
    else:
        def h_rows(b0, cols=slice(None)):
            return h_ref[pl.ds(b0, gsz), :, cols]

    @pl.when(n == 0)
    def _init():
        kbuf[:, 0:WINDOW] = kwin_ref[...]
        vbuf[:, 0:WINDOW] = vwin_ref[...]
        if tcp < WINDOW:
            kbuf[:, WINDOW:2 * WINDOW] = jnp.zeros((bblk, WINDOW, LANES), F32)
            vbuf[:, WINDOW:2 * WINDOW] = jnp.zeros((bblk, WINDOW, LANES), F32)
        sre[...] = sre0_ref[...]
        sim[...] = sim0_ref[...]
        r_in = r0_ref[...].reshape(bblk, N_RET_HEADS // 2, LANES, HEAD_DIM)
        row_d = lax.broadcasted_iota(jnp.int32, (LANES, LANES), 0) < HEAD_DIM
        col_d = lax.broadcasted_iota(jnp.int32, (LANES, LANES), 1) < HEAD_DIM
        rst[...] = jnp.where(row_d == col_d, jnp.concatenate([r_in, r_in], axis=-1), 0.0)
        lr = lam_ref[0:1, :]
        li = lam_ref[1:2, :]
        dt = jnp.exp(lam_ref[2:3, :])
        mag = jnp.exp(lr * dt)
        a_re = mag * jnp.cos(li * dt)
        a_im = mag * jnp.sin(li * dt)
        nr = a_re - 1.0
        ni = a_im
        den = lr * lr + li * li
        coef_re = (nr * lr + ni * li) / den
        coef_im = (ni * lr - nr * li) / den
        a_s[0:1, :] = a_re
        a_s[1:2, :] = a_im
        br = bbd_ref[0]
        bi = bbd_ref[1]
        wb_s[:, 0:D_STATE] = (coef_re * br - coef_im * bi).astype(BF16)
        wb_s[:, D_STATE:2 * D_STATE] = (coef_re * bi + coef_im * br).astype(BF16)
        if tc < tcp:
            yb_s[...] = jnp.zeros_like(yb_s)

    def lane_lo(rows):
        return lax.broadcasted_iota(jnp.int32, (rows, LANES), 1) < HEAD_DIM

    lo_q = lane_lo(tcp)
    lo_k = lane_lo(2 * WINDOW)
    row_w = lax.broadcasted_iota(jnp.int32, (WINDOW, LANES), 0)
    blockdiag = (row_w < HEAD_DIM) == lane_lo(WINDOW)
    headsum = jnp.where(blockdiag, 1.0 / HEAD_DIM, 0.0).astype(BF16)
    headsum2 = jnp.concatenate([headsum, headsum], axis=0)
    ones_k = jnp.ones((2 * WINDOW, LANES), BF16)

    cos_a, sa_hi, sa_lo = rope_ref[0], rope_ref[1], rope_ref[2]
    cos_r, sr_hi, sr_lo = rope_ref[3], rope_ref[4], rope_ref[5]

    def rope_a(x):
        return (x * cos_a + pltpu.roll(x, ROT_DIM // 2, 1) * sa_hi
                + pltpu.roll(x, LANES - ROT_DIM // 2, 1) * sa_lo)

    def rope_r(x):
        return (x * cos_r + pltpu.roll(x, HEAD_DIM // 2, 1) * sr_hi
                + pltpu.roll(x, LANES - HEAD_DIM // 2, 1) * sr_lo)

    def kv_variants(x):
        xr = pltpu.roll(x, HEAD_DIM, 1)
        return (jnp.where(lo_k, x, 0.0), jnp.where(lo_k, 0.0, xr),
                jnp.where(lo_k, xr, 0.0), jnp.where(lo_k, 0.0, x))

    row2 = lax.broadcasted_iota(jnp.int32, (2 * tcp, 2 * WINDOW), 0)
    qi = jnp.where(row2 < tcp, row2, row2 - tcp)
    kl = lax.broadcasted_iota(jnp.int32, (2 * tcp, 2 * WINDOW), 1)
    lmin = WINDOW - pos0 - n * tc
    valid = (kl > qi) & (kl <= qi + WINDOW) & (kl >= lmin)
    first_head = lax.broadcasted_iota(jnp.int32, (2 * tcp, 1), 0) < tcp

    def norm_body(g, carry):
        b0 = pl.multiple_of(g * gsz, gsz)
        r0 = pl.multiple_of(g * ROW_GROUP, ROW_GROUP)
        x = h_rows(b0).reshape(ROW_GROUP, D_MODEL)
        ms = jnp.mean(x * x, axis=-1, keepdims=True)
        hn = (x * lax.rsqrt(ms + EPS) * ng_ref[...]).astype(BF16)
        hn_s[pl.ds(r0, ROW_GROUP), :] = hn
        ub = jnp.dot(hn, win_ref[:, COL_B:COL_B + D_SSM], preferred_element_type=F32)
        u_s[0, pl.ds(r0, ROW_GROUP), :] = ub[:, 0:LANES]
        u_s[1, pl.ds(r0, ROW_GROUP), :] = ub[:, LANES:2 * LANES]
        gb_s[pl.ds(r0, ROW_GROUP), :] = jnp.dot(hn, win_ref[:, COL_B + D_SSM:COL_C],
                                                preferred_element_type=F32)
        return carry

    lax.fori_loop(0, n_groups, norm_body, 0)

    a_re_b = jnp.broadcast_to(a_s[0:1, :], (bblk, D_STATE))
    a_im_b = jnp.broadcast_to(a_s[1:2, :], (bblk, D_STATE))

    def s5_sub_block(si):
        t0 = si * sb
        for t in range(sb):
            for s in range(2):
                ut_s[t * bblk:(t + 1) * bblk, s * LANES:(s + 1) * LANES] = (
                    u_s[s, pl.ds(t0 + t, bblk, stride=tcp), :])
        for c in range(0, 2 * D_STATE, MXU_N):
            bu_s[:, c:c + MXU_N] = _mm(ut_s[...], wb_s[:, c:c + MXU_N])

        def step(t, xs):
            xr, xi = xs
            rt = pl.multiple_of(t * bblk, bblk)
            b_re = bu_s[pl.ds(rt, bblk), 0:D_STATE]
            b_im = bu_s[pl.ds(rt, bblk), D_STATE:2 * D_STATE]
            nxr = a_re_b * xr - a_im_b * xi + b_re
            nxi = a_re_b * xi + a_im_b * xr + b_im
            bu_s[pl.ds(rt, bblk), 0:D_STATE] = nxr
            bu_s[pl.ds(rt, bblk), D_STATE:2 * D_STATE] = nxi
            return nxr, nxi

        xr, xi = lax.fori_loop(0, sb, step, (sre[...], sim[...]), unroll=True)
        sre[...] = xr
        sim[...] = xi
        y = (_mm(bu_s[:, 0:D_STATE], wc_ref[0]) - _mm(bu_s[:, D_STATE:2 * D_STATE], wc_ref[1])
             + d_ref[...] * ut_s[...])
        y = jax.nn.gelu(y)
        y = y * jax.nn.sigmoid(_mm(y, wglu_ref[...]))
        for t in range(sb):
            for s in range(2):
                yb_s[s, pl.ds(t0 + t, bblk, stride=tcp), :] = y[t * bblk:(t + 1) * bblk, s * LANES:(s + 1) * LANES]

    def group_body(g, carry):
        b0 = pl.multiple_of(g * gsz, gsz)
        r0 = pl.multiple_of(g * ROW_GROUP, ROW_GROUP)
        hn = hn_s[pl.ds(r0, ROW_GROUP), :]
        for c in range(0, COL_B - COL_A, MXU_N):
            pa_s[:, c:c + MXU_N] = jnp.dot(hn, win_ref[:, COL_A + c:COL_A + c + MXU_N],
                                           preferred_element_type=F32)
        for c in range(0, D_IN - COL_C, MXU_N):
            pc_s[:, c:c + MXU_N] = jnp.dot(hn, win_ref[:, COL_C + c:COL_C + c + MXU_N],
                                           preferred_element_type=F32)

        def batch_body(j, carry2):
            b = b0 + j
            rj = pl.multiple_of(j * tcp, tcp)
            rb = pl.multiple_of(b * tcp, tcp)
            k_cur = rope_a(pa_s[pl.ds(rj, tcp), D_ATTN:D_ATTN + D_KV])
            v_cur = pa_s[pl.ds(rj, tcp), D_ATTN + D_KV:D_ATTN + 2 * D_KV]
            if tcp == WINDOW:
                kout_ref[b] = k_cur
                vout_ref[b] = v_cur
            else:
                keep = lax.broadcasted_iota(jnp.int32, (tcp, LANES), 0) < tcp - tc
                for buf, cur, out in ((kbuf, k_cur, kout_ref), (vbuf, v_cur, vout_ref)):
                    shifted = pltpu.roll(buf[b, 0:WINDOW], WINDOW - tc, 0)
                    out[b, 0:WINDOW - tcp] = shifted[0:WINDOW - tcp]
                    out[b, WINDOW - tcp:WINDOW] = jnp.where(keep, shifted[WINDOW - tcp:WINDOW],
                                                            pltpu.roll(cur, tcp - tc, 0))
            kbuf[b, WINDOW:WINDOW + tcp] = k_cur
            vbuf[b, WINDOW:WINDOW + tcp] = v_cur
            k_var = kv_variants(kbuf[b])
            v_var = kv_variants(vbuf[b])
            for kv in range(N_KV_HEADS):
                qs = jnp.concatenate(
                    [rope_a(pa_s[pl.ds(rj, tcp), (2 * kv + t) * LANES:(2 * kv + t + 1) * LANES]) for t in range(2)],
                    axis=0)
                qs = (qs * (HEAD_DIM ** -0.5 * LOG2E)).astype(BF16)
                o = jnp.zeros((2 * tcp, LANES), F32)
                for e in range(2):
                    vi = 2 * kv + e
                    s = jnp.where(valid, _mm_nt(qs, k_var[vi]), -jnp.inf)
                    sk = jnp.where(first_head, sink_ref[layer0 + lyr, 4 * kv + e], sink_ref[layer0 + lyr, 4 * kv + 2 + e]) * LOG2E
                    m = jnp.maximum(jnp.max(s, axis=-1, keepdims=True), sk)
                    ex = jnp.exp2(s - m)
                    r = _mm(ex, jnp.concatenate([v_var[vi].astype(BF16), ones_k], axis=1))
                    den = r[:, LANES:2 * LANES] + jnp.exp2(sk - m)
                    o = o + r[:, 0:LANES] / den
                for t in range(2):
                    pj = 2 * kv + t
                    ga = pa_s[pl.ds(rj, tcp),
                              D_ATTN + 2 * D_KV + pj * LANES:D_ATTN + 2 * D_KV + (pj + 1) * LANES]
                    ya_s[pl.ds(rb, tcp), pj * LANES:(pj + 1) * LANES] = o[t * tcp:(t + 1) * tcp] * _silu(ga)
            if n_chunks > 1:
                kbuf[b, 0:WINDOW] = k_cur
                vbuf[b, 0:WINDOW] = v_cur
            for pr in range(N_RET_HEADS // 2):
                cs = slice(pr * LANES, (pr + 1) * LANES)
                q = rope_r(pc_s[pl.ds(rj, tcp), pr * LANES:(pr + 1) * LANES])
                k = rope_r(pc_s[pl.ds(rj, tcp), D_RET + pr * LANES:D_RET + (pr + 1) * LANES]) * (HEAD_DIM ** -0.5)
                v = pc_s[pl.ds(rj, tcp), 2 * D_RET + pr * LANES:2 * D_RET + (pr + 1) * LANES]
                gr = pc_s[pl.ds(rj, tcp), 3 * D_RET + pr * LANES:3 * D_RET + (pr + 1) * LANES]
                r_old = rst[b, pr]
                k_heads = jnp.concatenate([jnp.where(lo_q, k, 0.0), jnp.where(lo_q, 0.0, k)], axis=0)
                v_heads = jnp.concatenate([jnp.where(lo_q, v, 0.0), jnp.where(lo_q, 0.0, v)], axis=0)
                sc = _mm_nt(q, k_heads) * dmask_ref[pr]
                o = _mm(q, r_old) * rdec_ref[1, :, cs] + _mm(sc, v_heads)
                kd = k * rdec_ref[0, :, cs]
                if tcp < WINDOW:
                    pad = jnp.zeros((WINDOW - tcp, LANES), F32)
                    d_new = _mm(jnp.concatenate([kd, pad], axis=0).T, jnp.concatenate([v, pad], axis=0))
                else:
                    d_new = _mm(kd.T, v)
                rst[b, pr] = r_old * rdec_ref[2, 0:1, cs] + jnp.where(blockdiag, d_new, 0.0)
                o2 = o * o
                o2_hi = o2.astype(BF16)
                o2_lo = (o2 - o2_hi.astype(F32)).astype(BF16)
                msq = jnp.dot(jnp.concatenate([o2_hi, o2_lo], axis=1), headsum2, preferred_element_type=F32)
                yc_s[pl.ds(rb, tcp), pr * LANES:(pr + 1) * LANES] = (
                    o * lax.rsqrt(msq + EPS) * rg_ref[:, cs] * _silu(gr))
            return carry2

        lax.fori_loop(0, gsz, batch_body, 0, unroll=min(gsz, 8))
        s5_sub_block(g)
        return carry

    lax.fori_loop(0, n_groups, group_body, 0)

    def out_body(g, carry):
        b0 = pl.multiple_of(g * gsz, gsz)
        r0 = pl.multiple_of(g * ROW_GROUP, ROW_GROUP)
        ya = ya_s[pl.ds(r0, ROW_GROUP), :].astype(BF16)
        gate_b = _silu(gb_s[pl.ds(r0, ROW_GROUP), :])
        yb0 = (yb_s[0, pl.ds(r0, ROW_GROUP), :] * gate_b[:, 0:LANES]).astype(BF16)
        yb1 = (yb_s[1, pl.ds(r0, ROW_GROUP), :] * gate_b[:, LANES:2 * LANES]).astype(BF16)
        yc = yc_s[pl.ds(r0, ROW_GROUP), :].astype(BF16)
        for c in range(0, D_MODEL, MXU_N):
            cc = slice(c, c + MXU_N)
            y = (jnp.dot(ya, wout_ref[0:D_ATTN, cc], preferred_element_type=F32)
                 + jnp.dot(yb0, wout_ref[D_ATTN:D_ATTN + LANES, cc], preferred_element_type=F32)
                 + jnp.dot(yb1, wout_ref[D_ATTN + LANES:D_ATTN + D_SSM, cc], preferred_element_type=F32)
                 + jnp.dot(yc, wout_ref[D_ATTN + D_SSM:D_MODEL, cc], preferred_element_type=F32))
            h1 = h_rows(b0, cc).reshape(ROW_GROUP, MXU_N) + y
            h1_s[:, cc] = h1
            h1b_s[:, cc] = h1.astype(BF16)
        pb = p_ref[pl.ds(b0, gsz)].reshape(ROW_GROUP, PLE_DIM).astype(BF16)
        ssq = jnp.zeros((ROW_GROUP, 1), F32)
        for c in range(0, D_MODEL, MXU_N):
            cc = slice(c, c + MXU_N)
            gate = jax.nn.sigmoid(jnp.dot(h1b_s[...], wgate_ref[:, cc], preferred_element_type=F32))
            pp = jnp.dot(pb, wpp_ref[:, cc], preferred_element_type=F32)
            h2 = h1_s[:, cc] + gate * pp
            if final_norm:
                ssq = ssq + jnp.sum(h2 * h2, axis=-1, keepdims=True)
                h1_s[:, cc] = h2
            else:
                hout_ref[pl.ds(b0, gsz), :, cc] = h2.reshape(gsz, tcp, MXU_N)

        def write_normed():
            scale = lax.rsqrt(ssq * (1.0 / D_MODEL) + EPS)
            hout_ref[pl.ds(b0, gsz)] = (h1_s[...] * scale * fg_ref[...]).reshape(gsz, tcp, D_MODEL)

        if resident:
            h_all[pl.ds(hb + b0, gsz)] = h1_s[...].reshape(gsz, tcp, D_MODEL)
            pl.when(lyr == n_layers - 1)(write_normed)
        elif final_norm:
            write_normed()
        return carry

    lax.fori_loop(0, n_groups, out_body, 0)

    @pl.when(n == n_chunks - 1)
    def _finish():
        sreo_ref[...] = sre[...]
        simo_ref[...] = sim[...]
        r_fin = rst[...]
        ro_ref[...] = (r_fin[..., 0:HEAD_DIM] + r_fin[..., HEAD_DIM:LANES]).reshape(
            bblk, N_RET_HEADS * HEAD_DIM, HEAD_DIM)


def _rope_tables(pos0, t_pad):
    pos = jnp.arange(t_pad, dtype=F32) + pos0
    lane = np.arange(LANES) % HEAD_DIM

    def tables(theta, rot_dim):
        half = rot_dim // 2
        inv = 1.0 / jnp.power(theta, jnp.arange(half, dtype=F32) / half)
        ang = pos[:, None] * inv[None, :]
        cos = jnp.cos(ang)
        sin = jnp.sin(ang)
        idx = np.where(lane < rot_dim, lane % half, 0)
        in_rot = jnp.asarray(lane < rot_dim)
        is_hi = jnp.asarray((lane >= half) & (lane < rot_dim))
        is_lo = jnp.asarray(lane < half)
        cos_l = jnp.where(in_rot[None, :], cos[:, idx], 1.0)
        sin_l = sin[:, idx]
        return (cos_l, jnp.where(is_hi[None, :], sin_l, 0.0), jnp.where(is_lo[None, :], -sin_l, 0.0))

    return jnp.stack(tables(ROPE_THETA, ROT_DIM) + tables(RET_THETA, HEAD_DIM))


def _retention_tables(tc, tcp):
    gammas = 1.0 - jnp.power(2.0, -5.0 - jnp.arange(N_RET_HEADS, dtype=F32))
    log_g = jnp.log(gammas)
    i = jnp.arange(tcp, dtype=F32)
    rel = i[:, None] - i[None, :]
    dmask = jnp.where(rel[None] >= 0, jnp.exp(log_g[:, None, None] * jnp.maximum(rel, 0.0)[None]), 0.0)
    kdec = jnp.exp(log_g[None, :] * (tc - 1.0 - i)[:, None])
    qdec = jnp.exp(log_g[None, :] * (i + 1.0)[:, None])
    gch = jnp.broadcast_to(jnp.exp(log_g * tc)[None, :], (tcp, N_RET_HEADS))
    rdec = jnp.stack([jnp.repeat(t, HEAD_DIM, axis=1) for t in (kdec, qdec, gch)])
    dmask = dmask.reshape(N_RET_HEADS // 2, 2, tcp, tcp).transpose(0, 2, 1, 3).reshape(
        N_RET_HEADS // 2, tcp, 2 * tcp)
    return dmask, rdec


def _layer_call(cfg, batch, h, p_all, rope, dmask, rdec, kwin, vwin, sre0, sim0, r0, weights):
    bblk, tc, tcp, n_chunks, pos0, layer0, n_layers, final_norm = cfg
    resident_h = n_layers > 1
    assert not resident_h or (n_chunks == 1 and final_norm)
    rows = bblk * tcp
    rows_sb = min(tc, ROW_GROUP // bblk) * bblk
    nb = batch // bblk
    t_pad = n_chunks * tcp
    const = pl.Buffered(1)

    def wspec(shape):
        nd = len(shape)
        return pl.BlockSpec((None,) + tuple(shape[1:]), lambda l, bb, n: (layer0 + l,) + (0,) * (nd - 1),
                            pipeline_mode=const)

    def sspec(shape, lead_layer):
        nd = len(shape)
        if lead_layer:
            return pl.BlockSpec((None, bblk) + tuple(shape[2:]),
                                lambda l, bb, n: (layer0 + l, bb) + (0,) * (nd - 2))
        return pl.BlockSpec((bblk,) + tuple(shape[1:]), lambda l, bb, n: (bb,) + (0,) * (nd - 1))

    def ospec(tail):
        return pl.BlockSpec((None, bblk) + tail, lambda l, bb, n: (l, bb) + (0,) * len(tail))

    if resident_h:
        h_in_map = lambda l, bb, n: (jnp.where(l == 0, bb, 0), 0, 0)
        h_out_map = lambda l, bb, n: (jnp.where(l == n_layers - 1, bb, 0), 0, 0)
    else:
        h_in_map = h_out_map = lambda l, bb, n: (bb, n, 0)

    (sink, ng, w_in, lam, bbd, wc, d_row, wglu, rg, w_out, w_pp, w_gate, fg) = weights
    in_specs = [
        pl.BlockSpec((bblk, tcp, D_MODEL), h_in_map),
        pl.BlockSpec((None, bblk, tcp, PLE_DIM), lambda l, bb, n: (layer0 + l, bb, n, 0)),
        pl.BlockSpec((6, tcp, LANES), lambda l, bb, n: (0, n, 0)),
        pl.BlockSpec(dmask.shape, lambda l, bb, n: (0, 0, 0), pipeline_mode=const),
        pl.BlockSpec(rdec.shape, lambda l, bb, n: (0, 0, 0), pipeline_mode=const),
        sspec(kwin.shape, kwin.ndim == 4), sspec(vwin.shape, vwin.ndim == 4),
        sspec(sre0.shape, sre0.ndim == 3), sspec(sim0.shape, sim0.ndim == 3),
        sspec(r0.shape, r0.ndim == 4),
        pl.BlockSpec(memory_space=pltpu.SMEM),
        wspec(ng.shape), wspec(w_in.shape), wspec(lam.shape), wspec(bbd.shape), wspec(wc.shape),
        wspec(d_row.shape), wspec(wglu.shape), wspec(rg.shape), wspec(w_out.shape), wspec(w_pp.shape),
        wspec(w_gate.shape),
        pl.BlockSpec(fg.shape, lambda l, bb, n: (0, 0), pipeline_mode=const),
    ]
    out_shape = (
        jax.ShapeDtypeStruct((batch, t_pad, D_MODEL), F32),
        jax.ShapeDtypeStruct((n_layers, batch, WINDOW, LANES), F32),
        jax.ShapeDtypeStruct((n_layers, batch, WINDOW, LANES), F32),
        jax.ShapeDtypeStruct((n_layers, batch, D_STATE), F32),
        jax.ShapeDtypeStruct((n_layers, batch, D_STATE), F32),
        jax.ShapeDtypeStruct((n_layers, batch, N_RET_HEADS * HEAD_DIM, HEAD_DIM), F32),
    )
    out_specs = (
        pl.BlockSpec((bblk, tcp, D_MODEL), h_out_map),
        ospec((WINDOW, LANES)), ospec((WINDOW, LANES)), ospec((D_STATE,)), ospec((D_STATE,)),
        ospec((N_RET_HEADS * HEAD_DIM, HEAD_DIM)),
    )
    scratch = [
        pltpu.VMEM((bblk, 2 * WINDOW, LANES), F32),
        pltpu.VMEM((bblk, 2 * WINDOW, LANES), F32),
        pltpu.VMEM((bblk, D_STATE), F32),
        pltpu.VMEM((bblk, D_STATE), F32),
        pltpu.VMEM((bblk, 2, LANES, LANES), F32),
        pltpu.VMEM((D_SSM, 2 * D_STATE), BF16),
        pltpu.VMEM((2, D_STATE), F32),
        pltpu.VMEM((rows, D_MODEL), BF16),
        pltpu.VMEM((ROW_GROUP, COL_B - COL_A), F32),
        pltpu.VMEM((ROW_GROUP, D_IN - COL_C), F32),
        pltpu.VMEM((2, rows, LANES), F32),
        pltpu.VMEM((rows, D_SSM), F32),
        pltpu.VMEM((rows, D_ATTN), F32),
        pltpu.VMEM((rows, D_RET), F32),
        pltpu.VMEM((2, rows, LANES), F32),
        pltpu.VMEM((rows_sb, D_SSM), F32),
        pltpu.VMEM((rows_sb, 2 * D_STATE), F32),
        pltpu.VMEM((ROW_GROUP, D_MODEL), F32),
        pltpu.VMEM((ROW_GROUP, D_MODEL), BF16),
    ]
    if resident_h:
        scratch.append(pltpu.VMEM((batch, tcp, D_MODEL), F32))
    return pl.pallas_call(
        functools.partial(_layer_kernel, cfg),
        grid=(n_layers, nb, n_chunks),
        in_specs=in_specs,
        out_specs=out_specs,
        out_shape=out_shape,
        scratch_shapes=scratch,
        compiler_params=pltpu.CompilerParams(
            dimension_semantics=("arbitrary", "arbitrary", "arbitrary"),
            vmem_limit_bytes=VMEM_LIMIT),
        name="sample_layers" if resident_h else "prompt_layer_%d" % layer0,
    )(h, p_all, rope, dmask, rdec, kwin, vwin, sre0, sim0, r0,
      sink, ng, w_in, lam, bbd, wc, d_row, wglu, rg, w_out, w_pp, w_gate, fg)


def kernel(x_prompt, x_sample, p_prompt, p_sample, cache_win_k, cache_win_v, state_ssm_re, state_ssm_im,
           state_ret, norm_g, w_in, attn_sink, ssm_lam_re, ssm_lam_im, ssm_log_dt, ssm_b_re, ssm_b_im,
           ssm_c_re, ssm_c_im, ssm_d, w_glu, ret_norm_g, w_out, w_ple_proj, w_ple_gate, final_norm_g):
    depth = w_in.shape[0]
    bp, seq = x_prompt.shape[0], x_prompt.shape[1]
    bs, dec_seq = x_sample.shape[0], x_sample.shape[1]
    past_len = 8192

    same_group = jnp.asarray(np.arange(D_SSM)[:, None] // SSM_GROUP == np.arange(D_STATE)[None, :] // SSM_STATE)
    bbd = jnp.stack([
        jnp.where(same_group, jnp.tile(b.transpose(0, 1, 3, 2).reshape(depth, D_SSM, SSM_STATE),
                                       (1, 1, N_SSM_GROUPS)), 0.0)
        for b in (ssm_b_re, ssm_b_im)], axis=1)
    wc = jnp.stack([
        jnp.where(same_group.T, jnp.tile(c.transpose(0, 1, 3, 2).reshape(depth, D_STATE, SSM_GROUP),
                                         (1, 1, N_SSM_GROUPS)), 0.0).astype(BF16)
        for c in (ssm_c_re, ssm_c_im)], axis=1)
    lam = jnp.stack([ssm_lam_re.reshape(depth, D_STATE), ssm_lam_im.reshape(depth, D_STATE),
                     jnp.repeat(ssm_log_dt, SSM_STATE, axis=-1)], axis=1)
    weights = (
        attn_sink,
        norm_g.reshape(depth, 1, D_MODEL), w_in.astype(BF16), lam, bbd, wc,
        ssm_d.reshape(depth, 1, D_SSM), w_glu.astype(BF16), ret_norm_g.reshape(depth, 1, D_RET),
        w_out.astype(BF16), w_ple_proj.astype(BF16), w_ple_gate.astype(BF16),
        final_norm_g.reshape(1, D_MODEL),
    )

    tc_p = WINDOW
    rope_p = _rope_tables(0.0, seq)
    dmask_p, rdec_p = _retention_tables(tc_p, tc_p)
    zk = jnp.zeros((bp, WINDOW, LANES), F32)
    zs = jnp.zeros((bp, D_STATE), F32)
    zr = jnp.zeros((bp, N_RET_HEADS * HEAD_DIM, HEAD_DIM), F32)

    tcp_s = 16
    bblk_s = ROW_GROUP // tcp_s
    rope_s = _rope_tables(float(past_len), tcp_s)
    dmask_s, rdec_s = _retention_tables(dec_seq, tcp_s)
    pad_t = ((0, 0), (0, tcp_s - dec_seq), (0, 0))
    hs = jnp.pad(x_sample, pad_t)
    ps = jnp.pad(p_sample, ((0, 0),) + pad_t)
    kwin_s = cache_win_k.reshape(depth, bs, WINDOW, LANES)
    vwin_s = cache_win_v.reshape(depth, bs, WINDOW, LANES)
    sre_s = state_ssm_re.reshape(depth, bs, D_STATE)
    sim_s = state_ssm_im.reshape(depth, bs, D_STATE)
    r_s = state_ret.reshape(depth, bs, N_RET_HEADS * HEAD_DIM, HEAD_DIM)

    hp = x_prompt
    outs_p = []
    for i in range(depth):
        cfg_p = (bp, tc_p, tc_p, seq // tc_p, 0, i, 1, i == depth - 1)
        hp, *st = _layer_call(cfg_p, bp, hp, p_prompt, rope_p, dmask_p, rdec_p, zk, zk, zs, zs, zr, weights)
        outs_p.append(st)
    st_p = [jnp.concatenate(t, axis=0) for t in zip(*outs_p)]
    cfg_s = (bblk_s, dec_seq, tcp_s, 1, past_len, 0, depth, True)
    hs, *st_s = _layer_call(cfg_s, bs, hs, ps, rope_s, dmask_s, rdec_s, kwin_s, vwin_s, sre_s, sim_s, r_s, weights)

    def pack(st, b):
        k, v, sr, si, r = st
        return (k.reshape(depth, b, WINDOW, N_KV_HEADS, HEAD_DIM), v.reshape(depth, b, WINDOW, N_KV_HEADS, HEAD_DIM),
                sr.reshape(depth, b, N_SSM_GROUPS, SSM_STATE), si.reshape(depth, b, N_SSM_GROUPS, SSM_STATE),
                r.reshape(depth, b, N_RET_HEADS, HEAD_DIM, HEAD_DIM))

    return (hp, hs[:, :dec_seq]) + pack(st_p, bp) + pack(st_s, bs)
```

```python
import functools
import math

import numpy as np
import jax
import jax.numpy as jnp
from jax import lax
from jax.experimental import pallas as pl
from jax.experimental.pallas import tpu as pltpu

F32 = jnp.float32
BF16 = jnp.bfloat16

D_MODEL = 1024
HEAD_DIM = 64
D_ATTN = 512
N_Q_HEADS = 8
N_KV_HEADS = 2
Q_PER_KV = N_Q_HEADS // N_KV_HEADS
D_KV = 128
WINDOW = 128
ROPE_THETA = 500000.0
ROT_DIM = 16
D_SSM = 256
SSM_GROUP = 16
N_SSM_GROUPS = 16
SSM_STATE = 64
D_STATE = N_SSM_GROUPS * SSM_STATE
D_RET = 256
N_RET_HEADS = 4
RET_THETA = 10000.0
PLE_DIM = 256
EPS = 1e-6
LOG2E = math.log2(math.e)
D_IN = 2816
COL_A = 0
COL_B = 1280
COL_C = 1792
LANES = 128
MXU_N = 256
ROW_GROUP = 256
VMEM_LIMIT = 58 * 1024 * 1024
GROUPS_PROMPT = (2, 2)
GROUPS_SAMPLE = (8, 4)


def _mm(a, b):
    return jnp.dot(a.astype(BF16), b.astype(BF16), preferred_element_type=F32)


def _mm_nt(a, b):
    return lax.dot_general(a.astype(BF16), b.astype(BF16), (((1,), (1,)), ((), ())),
                           preferred_element_type=F32)


def _silu(x):
    return x * jax.nn.sigmoid(x)


def _layer_kernel(cfg,
                  h_ref, p_ref, rope_ref, dmask_ref, rdec_ref,
                  kwin_ref, vwin_ref, sre0_ref, sim0_ref, r0_ref,
                  sink_ref, ng_ref, win_ref, lam_ref, bbd_ref, wc_ref, d_ref, wglu_ref, rg_ref,
                  wout_ref, wpp_ref, wgate_ref, fg_ref,
                  hout_ref, kout_ref, vout_ref, sreo_ref, simo_ref, ro_ref,
                  kbuf, vbuf, sre, sim, rst, wb_s, a_s, hn_s, pa_s, pc_s, u_s, gb_s,
                  ya_s, yc_s, yb_s, ut_s, bu_s, h1_s, h1b_s, *resident):
    bblk, tc, tcp, n_chunks, pos0, layer0, n_layers, final_norm, att_group, ret_group = cfg
    w_ret = ret_group * HEAD_DIM
    n_rg = N_RET_HEADS // ret_group
    gsz = ROW_GROUP // tcp
    n_groups = bblk // gsz
    sb = min(tc, ROW_GROUP // bblk)
    assert tc // sb == n_groups
    assert tcp & (tcp - 1) == 0
    lyr = pl.program_id(0)
    n = pl.program_id(2)
    if resident:
        h_all, = resident
        hb = pl.multiple_of(pl.program_id(1) * bblk, bblk)

        @pl.when(lyr == 0)
        def _load_h():
            h_all[pl.ds(hb, bblk)] = h_ref[...]

        def h_rows(b0, cols=slice(None)):
            return h_all[pl.ds(hb + b0, gsz), :, cols]
    else:
        def h_rows(b0, cols=slice(None)):
            return h_ref[pl.ds(b0, gsz), :, cols]

    def head_of(shape, dim):
        return lax.broadcasted_iota(jnp.int32, shape, dim) >> (HEAD_DIM.bit_length() - 1)

    same_head = head_of((w_ret, w_ret), 0) == head_of((w_ret, w_ret), 1)

    @pl.when(n == 0)
    def _init():
        kbuf[:, 0:WINDOW] = kwin_ref[...]
        vbuf[:, 0:WINDOW] = vwin_ref[...]
        if tcp < WINDOW:
            kbuf[:, WINDOW:2 * WINDOW] = jnp.zeros((bblk, WINDOW, LANES), F32)
            vbuf[:, WINDOW:2 * WINDOW] = jnp.zeros((bblk, WINDOW, LANES), F32)
        sre[...] = sre0_ref[...]
        sim[...] = sim0_ref[...]
        r_in = r0_ref[...].reshape(bblk, n_rg, w_ret, HEAD_DIM)
        rst[...] = jnp.where(same_head, jnp.concatenate([r_in] * ret_group, axis=-1), 0.0)
        lr = lam_ref[0:1, :]
        li = lam_ref[1:2, :]
        dt = jnp.exp(lam_ref[2:3, :])
        mag = jnp.exp(lr * dt)
        a_re = mag * jnp.cos(li * dt)
        a_im = mag * jnp.sin(li * dt)
        nr = a_re - 1.0
        ni = a_im
        den = lr * lr + li * li
        coef_re = (nr * lr + ni * li) / den
        coef_im = (ni * lr - nr * li) / den
        a_s[0:1, :] = a_re
        a_s[1:2, :] = a_im
        br = bbd_ref[0]
        bi = bbd_ref[1]
        wb_s[:, 0:D_STATE] = (coef_re * br - coef_im * bi).astype(BF16)
        wb_s[:, D_STATE:2 * D_STATE] = (coef_re * bi + coef_im * br).astype(BF16)
        if tc < tcp:
            yb_s[...] = jnp.zeros_like(yb_s)

    lo_q = head_of((tcp, LANES), 1) == 0
    headmean = jnp.where(same_head, 1.0 / HEAD_DIM, 0.0).astype(BF16)
    headmean2 = jnp.concatenate([headmean, headmean], axis=0)
    ones_k = jnp.ones((2 * WINDOW, LANES), BF16)
    ret_lane_head = head_of((tcp, w_ret), 1)

    cos_a, sa_hi, sa_lo = rope_ref[0], rope_ref[1], rope_ref[2]
    cos_r, sr_hi, sr_lo = rope_ref[3], rope_ref[4], rope_ref[5]

    def rope_a(x):
        return (x * cos_a + pltpu.roll(x, ROT_DIM // 2, 1) * sa_hi
                + pltpu.roll(x, LANES - ROT_DIM // 2, 1) * sa_lo)

    def rope_r(x):
        return (x * cos_r + pltpu.roll(x, HEAD_DIM // 2, 1) * sr_hi
                + pltpu.roll(x, LANES - HEAD_DIM // 2, 1) * sr_lo)

    lmin = WINDOW - pos0 - n * tc

    def valid_keys(n_heads):
        qi = lax.broadcasted_iota(jnp.int32, (n_heads * tcp, 2 * WINDOW), 0) & (tcp - 1)
        kl = lax.broadcasted_iota(jnp.int32, (n_heads * tcp, 2 * WINDOW), 1)
        return (kl > qi) & (kl <= qi + WINDOW) & (kl >= lmin)

    valid = valid_keys(1)
    valid2 = valid_keys(2)
    first_head = lax.broadcasted_iota(jnp.int32, (2 * tcp, 1), 0) < tcp
    lo_k = head_of((2 * WINDOW, LANES), 1) == 0

    def norm_body(g, carry):
        b0 = pl.multiple_of(g * gsz, gsz)
        r0 = pl.multiple_of(g * ROW_GROUP, ROW_GROUP)
        x = h_rows(b0).reshape(ROW_GROUP, D_MODEL)
        ms = jnp.mean(x * x, axis=-1, keepdims=True)
        hn = (x * lax.rsqrt(ms + EPS) * ng_ref[...]).astype(BF16)
        hn_s[pl.ds(r0, ROW_GROUP), :] = hn
        ub = jnp.dot(hn, win_ref[:, COL_B:COL_B + D_SSM], preferred_element_type=F32)
        u_s[0, pl.ds(r0, ROW_GROUP), :] = ub[:, 0:LANES]
        u_s[1, pl.ds(r0, ROW_GROUP), :] = ub[:, LANES:2 * LANES]
        gb_s[pl.ds(r0, ROW_GROUP), :] = jnp.dot(hn, win_ref[:, COL_B + D_SSM:COL_C],
                                                preferred_element_type=F32)
        return carry

    lax.fori_loop(0, n_groups, norm_body, 0)

    a_re_b = jnp.broadcast_to(a_s[0:1, :], (bblk, D_STATE))
    a_im_b = jnp.broadcast_to(a_s[1:2, :], (bblk, D_STATE))

    def s5_sub_block(si):
        t0 = si * sb
        for t in range(sb):
            for s in range(2):
                ut_s[t * bblk:(t + 1) * bblk, s * LANES:(s + 1) * LANES] = (
                    u_s[s, pl.ds(t0 + t, bblk, stride=tcp), :])
        for c in range(0, 2 * D_STATE, MXU_N):
            bu_s[:, c:c + MXU_N] = _mm(ut_s[...], wb_s[:, c:c + MXU_N])

        def step(t, xs):
            xr, xi = xs
            rt = pl.multiple_of(t * bblk, bblk)
            b_re = bu_s[pl.ds(rt, bblk), 0:D_STATE]
            b_im = bu_s[pl.ds(rt, bblk), D_STATE:2 * D_STATE]
            nxr = a_re_b * xr - a_im_b * xi + b_re
            nxi = a_re_b * xi + a_im_b * xr + b_im
            bu_s[pl.ds(rt, bblk), 0:D_STATE] = nxr
            bu_s[pl.ds(rt, bblk), D_STATE:2 * D_STATE] = nxi
            return nxr, nxi

        xr, xi = lax.fori_loop(0, sb, step, (sre[...], sim[...]), unroll=True)
        sre[...] = xr
        sim[...] = xi
        y = (_mm(bu_s[:, 0:D_STATE], wc_ref[0]) - _mm(bu_s[:, D_STATE:2 * D_STATE], wc_ref[1])
             + d_ref[...] * ut_s[...])
        y = jax.nn.gelu(y)
        y = y * jax.nn.sigmoid(_mm(y, wglu_ref[...]))
        for t in range(sb):
            for s in range(2):
                yb_s[s, pl.ds(t0 + t, bblk, stride=tcp), :] = y[t * bblk:(t + 1) * bblk, s * LANES:(s + 1) * LANES]

    def group_body(g, carry):
        b0 = pl.multiple_of(g * gsz, gsz)
        r0 = pl.multiple_of(g * ROW_GROUP, ROW_GROUP)
        hn = hn_s[pl.ds(r0, ROW_GROUP), :]
        for c in range(0, COL_B - COL_A, MXU_N):
            pa_s[:, c:c + MXU_N] = jnp.dot(hn, win_ref[:, COL_A + c:COL_A + c + MXU_N],
                                           preferred_element_type=F32)
        for c in range(0, D_IN - COL_C, MXU_N):
            pc_s[:, c:c + MXU_N] = jnp.dot(hn, win_ref[:, COL_C + c:COL_C + c + MXU_N],
                                           preferred_element_type=F32)

        def batch_body(j, carry2):
            b = b0 + j
            rj = pl.multiple_of(j * tcp, tcp)
            rb = pl.multiple_of(b * tcp, tcp)
            k_cur = rope_a(pa_s[pl.ds(rj, tcp), D_ATTN:D_ATTN + D_KV])
            v_cur = pa_s[pl.ds(rj, tcp), D_ATTN + D_KV:D_ATTN + 2 * D_KV]
            if tcp == WINDOW:
                kout_ref[b] = k_cur
                vout_ref[b] = v_cur
            else:
                keep = lax.broadcasted_iota(jnp.int32, (tcp, LANES), 0) < tcp - tc
                for buf, cur, out in ((kbuf, k_cur, kout_ref), (vbuf, v_cur, vout_ref)):
                    shifted = pltpu.roll(buf[b, 0:WINDOW], WINDOW - tc, 0)
                    out[b, 0:WINDOW - tcp] = shifted[0:WINDOW - tcp]
                    out[b, WINDOW - tcp:WINDOW] = jnp.where(keep, shifted[WINDOW - tcp:WINDOW],
                                                            pltpu.roll(cur, tcp - tc, 0))
            kbuf[b, WINDOW:WINDOW + tcp] = k_cur
            vbuf[b, WINDOW:WINDOW + tcp] = v_cur
            def q_pair(pj):
                return rope_a(pa_s[pl.ds(rj, tcp), pj * LANES:(pj + 1) * LANES])

            def softmax_matmul(q_grp, heads, k_op, v_op):
                s_grp = _mm_nt(q_grp * (HEAD_DIM ** -0.5 * LOG2E), k_op)
                ex, sink_w = [], []
                for t, hd in enumerate(heads):
                    s = jnp.where(valid, s_grp[t * tcp:(t + 1) * tcp], -jnp.inf)
                    sk = sink_ref[layer0 + lyr, hd] * LOG2E
                    m = jnp.maximum(jnp.max(s, axis=-1, keepdims=True), sk)
                    ex.append(jnp.exp2(s - m).astype(BF16))
                    sink_w.append(jnp.exp2(sk - m))
                r = _mm(jnp.concatenate(ex, axis=0), jnp.concatenate([v_op.astype(BF16), ones_k], axis=1))
                o_grp = r[:, 0:LANES] / (r[:, LANES:2 * LANES] + jnp.concatenate(sink_w, axis=0))
                return [o_grp[t * tcp:(t + 1) * tcp] for t in range(len(heads))]

            k_cat, v_cat = kbuf[b], vbuf[b]
            hi_q = jnp.logical_not(lo_q)
            o_heads = [None] * N_Q_HEADS
            if att_group == N_Q_HEADS:
                q_rows = []
                for pj in range(N_Q_HEADS // 2):
                    qp = q_pair(pj)
                    qx = pltpu.roll(qp, HEAD_DIM, 1)
                    kv_lo = (pj // 2 == 0)
                    q_even, q_odd = (qp, qx) if kv_lo else (qx, qp)
                    keep = lo_q if kv_lo else hi_q
                    q_rows += [jnp.where(keep, q_even, 0.0), jnp.where(keep, q_odd, 0.0)]
                o_heads = softmax_matmul(jnp.concatenate(q_rows, axis=0), range(N_Q_HEADS), k_cat, v_cat)
                for hd in range(N_Q_HEADS):
                    if (hd % 2 == 0) != (hd // Q_PER_KV == 0):
                        o_heads[hd] = pltpu.roll(o_heads[hd], HEAD_DIM, 1)
            else:
                k_rol, v_rol = pltpu.roll(k_cat, HEAD_DIM, 1), pltpu.roll(v_cat, HEAD_DIM, 1)
                for kv in range(N_KV_HEADS):
                    qs = jnp.concatenate([q_pair(2 * kv), q_pair(2 * kv + 1)], axis=0)
                    qs = (qs * (HEAD_DIM ** -0.5 * LOG2E)).astype(BF16)
                    o = jnp.zeros((2 * tcp, LANES), F32)
                    for e in range(2):
                        keep = lo_k if e == 0 else jnp.logical_not(lo_k)
                        k_op = jnp.where(keep, k_cat if e == kv else k_rol, 0.0)
                        v_op = jnp.where(keep, v_cat if e == kv else v_rol, 0.0)
                        s = jnp.where(valid2, _mm_nt(qs, k_op), -jnp.inf)
                        sk = jnp.where(first_head, sink_ref[layer0 + lyr, 4 * kv + e],
                                       sink_ref[layer0 + lyr, 4 * kv + 2 + e]) * LOG2E
                        m = jnp.maximum(jnp.max(s, axis=-1, keepdims=True), sk)
                        r = _mm(jnp.exp2(s - m), jnp.concatenate([v_op.astype(BF16), ones_k], axis=1))
                        o = o + r[:, 0:LANES] / (r[:, LANES:2 * LANES] + jnp.exp2(sk - m))
                    o_heads[4 * kv:4 * kv + 4] = [o[0:tcp], o[0:tcp], o[tcp:2 * tcp], o[tcp:2 * tcp]]
            for pj in range(N_Q_HEADS // 2):
                ga = pa_s[pl.ds(rj, tcp),
                          D_ATTN + 2 * D_KV + pj * LANES:D_ATTN + 2 * D_KV + (pj + 1) * LANES]
                ya_s[pl.ds(rb, tcp), pj * LANES:(pj + 1) * LANES] = (
                    jnp.where(lo_q, o_heads[2 * pj], o_heads[2 * pj + 1]) * _silu(ga))
            if n_chunks > 1:
                kbuf[b, 0:WINDOW] = k_cur
                vbuf[b, 0:WINDOW] = v_cur
            for pr in range(n_rg):
                cs = slice(pr * w_ret, (pr + 1) * w_ret)

                def rope_cols(c0):
                    return jnp.concatenate([rope_r(pc_s[pl.ds(rj, tcp), c0 + t * LANES:c0 + (t + 1) * LANES])
                                            for t in range(w_ret // LANES)], axis=1)

                q = rope_cols(pr * w_ret)
                k = rope_cols(D_RET + pr * w_ret) * (HEAD_DIM ** -0.5)
                v = pc_s[pl.ds(rj, tcp), 2 * D_RET + pr * w_ret:2 * D_RET + (pr + 1) * w_ret]
                gr = pc_s[pl.ds(rj, tcp), 3 * D_RET + pr * w_ret:3 * D_RET + (pr + 1) * w_ret]
                r_old = rst[b, pr]
                k_heads = jnp.concatenate([jnp.where(ret_lane_head == hd, k, 0.0) for hd in range(ret_group)], axis=0)
                v_heads = jnp.concatenate([jnp.where(ret_lane_head == hd, v, 0.0) for hd in range(ret_group)], axis=0)
                sc = _mm_nt(q, k_heads) * dmask_ref[pr]
                o = _mm(q, r_old) * rdec_ref[1, :, cs] + _mm(sc, v_heads)
                kd = k * rdec_ref[0, :, cs]
                if tcp < WINDOW:
                    pad = jnp.zeros((WINDOW - tcp, w_ret), F32)
                    d_new = _mm(jnp.concatenate([kd, pad], axis=0).T, jnp.concatenate([v, pad], axis=0))
                else:
                    d_new = _mm(kd.T, v)
                rst[b, pr] = r_old * rdec_ref[2, 0:1, cs] + jnp.where(same_head, d_new, 0.0)
                o2 = o * o
                o2_hi = o2.astype(BF16)
                o2_lo = (o2 - o2_hi.astype(F32)).astype(BF16)
                msq = jnp.dot(jnp.concatenate([o2_hi, o2_lo], axis=1), headmean2, preferred_element_type=F32)
                yc_s[pl.ds(rb, tcp), cs] = o * lax.rsqrt(msq + EPS) * rg_ref[:, cs] * _silu(gr)
            return carry2

        lax.fori_loop(0, gsz, batch_body, 0, unroll=min(gsz, 8))
        s5_sub_block(g)
        return carry

    lax.fori_loop(0, n_groups, group_body, 0)

    def out_body(g, carry):
        b0 = pl.multiple_of(g * gsz, gsz)
        r0 = pl.multiple_of(g * ROW_GROUP, ROW_GROUP)
        ya = ya_s[pl.ds(r0, ROW_GROUP), :].astype(BF16)
        gate_b = _silu(gb_s[pl.ds(r0, ROW_GROUP), :])
        yb0 = (yb_s[0, pl.ds(r0, ROW_GROUP), :] * gate_b[:, 0:LANES]).astype(BF16)
        yb1 = (yb_s[1, pl.ds(r0, ROW_GROUP), :] * gate_b[:, LANES:2 * LANES]).astype(BF16)
        yc = yc_s[pl.ds(r0, ROW_GROUP), :].astype(BF16)
        for c in range(0, D_MODEL, MXU_N):
            cc = slice(c, c + MXU_N)
            y = (jnp.dot(ya, wout_ref[0:D_ATTN, cc], preferred_element_type=F32)
                 + jnp.dot(yb0, wout_ref[D_ATTN:D_ATTN + LANES, cc], preferred_element_type=F32)
                 + jnp.dot(yb1, wout_ref[D_ATTN + LANES:D_ATTN + D_SSM, cc], preferred_element_type=F32)
                 + jnp.dot(yc, wout_ref[D_ATTN + D_SSM:D_MODEL, cc], preferred_element_type=F32))
            h1 = h_rows(b0, cc).reshape(ROW_GROUP, MXU_N) + y
            h1_s[:, cc] = h1
            h1b_s[:, cc] = h1.astype(BF16)
        pb = p_ref[pl.ds(b0, gsz)].reshape(ROW_GROUP, PLE_DIM).astype(BF16)
        ssq = jnp.zeros((ROW_GROUP, 1), F32)
        for c in range(0, D_MODEL, MXU_N):
            cc = slice(c, c + MXU_N)
            gate = jax.nn.sigmoid(jnp.dot(h1b_s[...], wgate_ref[:, cc], preferred_element_type=F32))
            pp = jnp.dot(pb, wpp_ref[:, cc], preferred_element_type=F32)
            h2 = h1_s[:, cc] + gate * pp
            if final_norm:
                ssq = ssq + jnp.sum(h2 * h2, axis=-1, keepdims=True)
                h1_s[:, cc] = h2
            else:
                hout_ref[pl.ds(b0, gsz), :, cc] = h2.reshape(gsz, tcp, MXU_N)

        def write_normed():
            scale = lax.rsqrt(ssq * (1.0 / D_MODEL) + EPS)
            hout_ref[pl.ds(b0, gsz)] = (h1_s[...] * scale * fg_ref[...]).reshape(gsz, tcp, D_MODEL)

        if resident:
            h_all[pl.ds(hb + b0, gsz)] = h1_s[...].reshape(gsz, tcp, D_MODEL)
            pl.when(lyr == n_layers - 1)(write_normed)
        elif final_norm:
            write_normed()
        return carry

    lax.fori_loop(0, n_groups, out_body, 0)

    @pl.when(n == n_chunks - 1)
    def _finish():
        sreo_ref[...] = sre[...]
        simo_ref[...] = sim[...]
        r_fin = rst[...]
        ro_ref[...] = sum(r_fin[..., hd * HEAD_DIM:(hd + 1) * HEAD_DIM] for hd in range(ret_group)).reshape(
            bblk, N_RET_HEADS * HEAD_DIM, HEAD_DIM)


def _rope_tables(pos0, t_pad):
    pos = jnp.arange(t_pad, dtype=F32) + pos0
    lane = np.arange(LANES) % HEAD_DIM

    def tables(theta, rot_dim):
        half = rot_dim // 2
        inv = 1.0 / jnp.power(theta, jnp.arange(half, dtype=F32) / half)
        ang = pos[:, None] * inv[None, :]
        cos = jnp.cos(ang)
        sin = jnp.sin(ang)
        idx = np.where(lane < rot_dim, lane % half, 0)
        in_rot = jnp.asarray(lane < rot_dim)
        is_hi = jnp.asarray((lane >= half) & (lane < rot_dim))
        is_lo = jnp.asarray(lane < half)
        cos_l = jnp.where(in_rot[None, :], cos[:, idx], 1.0)
        sin_l = sin[:, idx]
        return (cos_l, jnp.where(is_hi[None, :], sin_l, 0.0), jnp.where(is_lo[None, :], -sin_l, 0.0))

    return jnp.stack(tables(ROPE_THETA, ROT_DIM) + tables(RET_THETA, HEAD_DIM))


def _retention_tables(tc, tcp, ret_group):
    gammas = 1.0 - jnp.power(2.0, -5.0 - jnp.arange(N_RET_HEADS, dtype=F32))
    log_g = jnp.log(gammas)
    i = jnp.arange(tcp, dtype=F32)
    rel = i[:, None] - i[None, :]
    dmask = jnp.where(rel[None] >= 0, jnp.exp(log_g[:, None, None] * jnp.maximum(rel, 0.0)[None]), 0.0)
    kdec = jnp.exp(log_g[None, :] * (tc - 1.0 - i)[:, None])
    qdec = jnp.exp(log_g[None, :] * (i + 1.0)[:, None])
    gch = jnp.broadcast_to(jnp.exp(log_g * tc)[None, :], (tcp, N_RET_HEADS))
    rdec = jnp.stack([jnp.repeat(t, HEAD_DIM, axis=1) for t in (kdec, qdec, gch)])
    dmask = dmask.reshape(N_RET_HEADS // ret_group, ret_group, tcp, tcp).transpose(0, 2, 1, 3).reshape(
        N_RET_HEADS // ret_group, tcp, ret_group * tcp)
    return dmask, rdec


def _layer_call(cfg, batch, h, p_all, rope, dmask, rdec, kwin, vwin, sre0, sim0, r0, weights):
    bblk, tc, tcp, n_chunks, pos0, layer0, n_layers, final_norm, att_group, ret_group = cfg
    w_ret = ret_group * HEAD_DIM
    resident_h = n_layers > 1
    assert not resident_h or (n_chunks == 1 and final_norm)
    rows = bblk * tcp
    rows_sb = min(tc, ROW_GROUP // bblk) * bblk
    nb = batch // bblk
    t_pad = n_chunks * tcp
    const = pl.Buffered(1)

    def wspec(shape):
        nd = len(shape)
        return pl.BlockSpec((None,) + tuple(shape[1:]), lambda l, bb, n: (layer0 + l,) + (0,) * (nd - 1),
                            pipeline_mode=const)

    def sspec(shape, lead_layer, mode=None):
        nd = len(shape)
        if lead_layer:
            return pl.BlockSpec((None, bblk) + tuple(shape[2:]),
                                lambda l, bb, n: (layer0 + l, bb) + (0,) * (nd - 2), pipeline_mode=mode)
        return pl.BlockSpec((bblk,) + tuple(shape[1:]), lambda l, bb, n: (bb,) + (0,) * (nd - 1),
                            pipeline_mode=mode)

    big_in_mode = const if resident_h else None

    def ospec(tail):
        return pl.BlockSpec((None, bblk) + tail, lambda l, bb, n: (l, bb) + (0,) * len(tail))

    if resident_h:
        h_in_map = lambda l, bb, n: (jnp.where(l == 0, bb, 0), 0, 0)
        h_out_map = lambda l, bb, n: (jnp.where(l == n_layers - 1, bb, 0), 0, 0)
    else:
        h_in_map = h_out_map = lambda l, bb, n: (bb, n, 0)

    (sink, ng, w_in, lam, bbd, wc, d_row, wglu, rg, w_out, w_pp, w_gate, fg) = weights
    in_specs = [
        pl.BlockSpec((bblk, tcp, D_MODEL), h_in_map, pipeline_mode=big_in_mode),
        pl.BlockSpec((None, bblk, tcp, PLE_DIM), lambda l, bb, n: (layer0 + l, bb, n, 0)),
        pl.BlockSpec((6, tcp, LANES), lambda l, bb, n: (0, n, 0)),
        pl.BlockSpec(dmask.shape, lambda l, bb, n: (0, 0, 0), pipeline_mode=const),
        pl.BlockSpec(rdec.shape, lambda l, bb, n: (0, 0, 0), pipeline_mode=const),
        sspec(kwin.shape, kwin.ndim == 4), sspec(vwin.shape, vwin.ndim == 4),
        sspec(sre0.shape, sre0.ndim == 3), sspec(sim0.shape, sim0.ndim == 3),
        sspec(r0.shape, r0.ndim == 4, big_in_mode),
        pl.BlockSpec(memory_space=pltpu.SMEM),
        wspec(ng.shape), wspec(w_in.shape), wspec(lam.shape), wspec(bbd.shape), wspec(wc.shape),
        wspec(d_row.shape), wspec(wglu.shape), wspec(rg.shape), wspec(w_out.shape), wspec(w_pp.shape),
        wspec(w_gate.shape),
        pl.BlockSpec(fg.shape, lambda l, bb, n: (0, 0), pipeline_mode=const),
    ]
    out_shape = (
        jax.ShapeDtypeStruct((batch, t_pad, D_MODEL), F32),
        jax.ShapeDtypeStruct((n_layers, batch, WINDOW, LANES), F32),
        jax.ShapeDtypeStruct((n_layers, batch, WINDOW, LANES), F32),
        jax.ShapeDtypeStruct((n_layers, batch, D_STATE), F32),
        jax.ShapeDtypeStruct((n_layers, batch, D_STATE), F32),
        jax.ShapeDtypeStruct((n_layers, batch, N_RET_HEADS * HEAD_DIM, HEAD_DIM), F32),
    )
    out_specs = (
        pl.BlockSpec((bblk, tcp, D_MODEL), h_out_map),
        ospec((WINDOW, LANES)), ospec((WINDOW, LANES)), ospec((D_STATE,)), ospec((D_STATE,)),
        ospec((N_RET_HEADS * HEAD_DIM, HEAD_DIM)),
    )
    scratch = [
        pltpu.VMEM((bblk, 2 * WINDOW, LANES), F32),
        pltpu.VMEM((bblk, 2 * WINDOW, LANES), F32),
        pltpu.VMEM((bblk, D_STATE), F32),
        pltpu.VMEM((bblk, D_STATE), F32),
        pltpu.VMEM((bblk, N_RET_HEADS // ret_group, w_ret, w_ret), F32),
        pltpu.VMEM((D_SSM, 2 * D_STATE), BF16),
        pltpu.VMEM((2, D_STATE), F32),
        pltpu.VMEM((rows, D_MODEL), BF16),
        pltpu.VMEM((ROW_GROUP, COL_B - COL_A), F32),
        pltpu.VMEM((ROW_GROUP, D_IN - COL_C), F32),
        pltpu.VMEM((2, rows, LANES), F32),
        pltpu.VMEM((rows, D_SSM), F32),
        pltpu.VMEM((rows, D_ATTN), F32),
        pltpu.VMEM((rows, D_RET), F32),
        pltpu.VMEM((2, rows, LANES), F32),
        pltpu.VMEM((rows_sb, D_SSM), F32),
        pltpu.VMEM((rows_sb, 2 * D_STATE), F32),
        pltpu.VMEM((ROW_GROUP, D_MODEL), F32),
        pltpu.VMEM((ROW_GROUP, D_MODEL), BF16),
    ]
    if resident_h:
        scratch.append(pltpu.VMEM((batch, tcp, D_MODEL), F32))
    return pl.pallas_call(
        functools.partial(_layer_kernel, cfg),
        grid=(n_layers, nb, n_chunks),
        in_specs=in_specs,
        out_specs=out_specs,
        out_shape=out_shape,
        scratch_shapes=scratch,
        compiler_params=pltpu.CompilerParams(
            dimension_semantics=("arbitrary", "arbitrary", "arbitrary"),
            vmem_limit_bytes=VMEM_LIMIT),
        name="sample_layers" if resident_h else "prompt_layer_%d" % layer0,
    )(h, p_all, rope, dmask, rdec, kwin, vwin, sre0, sim0, r0,
      sink, ng, w_in, lam, bbd, wc, d_row, wglu, rg, w_out, w_pp, w_gate, fg)


def kernel(x_prompt, x_sample, p_prompt, p_sample, cache_win_k, cache_win_v, state_ssm_re, state_ssm_im,
           state_ret, norm_g, w_in, attn_sink, ssm_lam_re, ssm_lam_im, ssm_log_dt, ssm_b_re, ssm_b_im,
           ssm_c_re, ssm_c_im, ssm_d, w_glu, ret_norm_g, w_out, w_ple_proj, w_ple_gate, final_norm_g):
    depth = w_in.shape[0]
    bp, seq = x_prompt.shape[0], x_prompt.shape[1]
    bs, dec_seq = x_sample.shape[0], x_sample.shape[1]
    past_len = 8192

    same_group = jnp.asarray(np.arange(D_SSM)[:, None] // SSM_GROUP == np.arange(D_STATE)[None, :] // SSM_STATE)
    bbd = jnp.stack([
        jnp.where(same_group, jnp.tile(b.transpose(0, 1, 3, 2).reshape(depth, D_SSM, SSM_STATE),
                                       (1, 1, N_SSM_GROUPS)), 0.0)
        for b in (ssm_b_re, ssm_b_im)], axis=1)
    wc = jnp.stack([
        jnp.where(same_group.T, jnp.tile(c.transpose(0, 1, 3, 2).reshape(depth, D_STATE, SSM_GROUP),
                                         (1, 1, N_SSM_GROUPS)), 0.0).astype(BF16)
        for c in (ssm_c_re, ssm_c_im)], axis=1)
    lam = jnp.stack([ssm_lam_re.reshape(depth, D_STATE), ssm_lam_im.reshape(depth, D_STATE),
                     jnp.repeat(ssm_log_dt, SSM_STATE, axis=-1)], axis=1)
    weights = (
        attn_sink,
        norm_g.reshape(depth, 1, D_MODEL), w_in.astype(BF16), lam, bbd, wc,
        ssm_d.reshape(depth, 1, D_SSM), w_glu.astype(BF16), ret_norm_g.reshape(depth, 1, D_RET),
        w_out.astype(BF16), w_ple_proj.astype(BF16), w_ple_gate.astype(BF16),
        final_norm_g.reshape(1, D_MODEL),
    )

    tc_p = WINDOW
    rope_p = _rope_tables(0.0, seq)
    dmask_p, rdec_p = _retention_tables(tc_p, tc_p, GROUPS_PROMPT[1])
    zk = jnp.zeros((bp, WINDOW, LANES), F32)
    zs = jnp.zeros((bp, D_STATE), F32)
    zr = jnp.zeros((bp, N_RET_HEADS * HEAD_DIM, HEAD_DIM), F32)

    tcp_s = 16
    bblk_s = ROW_GROUP // tcp_s
    rope_s = _rope_tables(float(past_len), tcp_s)
    dmask_s, rdec_s = _retention_tables(dec_seq, tcp_s, GROUPS_SAMPLE[1])
    pad_t = ((0, 0), (0, tcp_s - dec_seq), (0, 0))
    hs = jnp.pad(x_sample, pad_t)
    ps = jnp.pad(p_sample, ((0, 0),) + pad_t)
    kwin_s = cache_win_k.reshape(depth, bs, WINDOW, LANES)
    vwin_s = cache_win_v.reshape(depth, bs, WINDOW, LANES)
    sre_s = state_ssm_re.reshape(depth, bs, D_STATE)
    sim_s = state_ssm_im.reshape(depth, bs, D_STATE)
    r_s = state_ret.reshape(depth, bs, N_RET_HEADS * HEAD_DIM, HEAD_DIM)

    hp = x_prompt
    outs_p = []
    for i in range(depth):
        cfg_p = (bp, tc_p, tc_p, seq // tc_p, 0, i, 1, i == depth - 1) + GROUPS_PROMPT
        hp, *st = _layer_call(cfg_p, bp, hp, p_prompt, rope_p, dmask_p, rdec_p, zk, zk, zs, zs, zr, weights)
        outs_p.append(st)
    st_p = [jnp.concatenate(t, axis=0) for t in zip(*outs_p)]
    cfg_s = (bblk_s, dec_seq, tcp_s, 1, past_len, 0, depth, True) + GROUPS_SAMPLE
    hs, *st_s = _layer_call(cfg_s, bs, hs, ps, rope_s, dmask_s, rdec_s, kwin_s, vwin_s, sre_s, sim_s, r_s, weights)

    def pack(st, b):
        k, v, sr, si, r = st
        return (k.reshape(depth, b, WINDOW, N_KV_HEADS, HEAD_DIM), v.reshape(depth, b, WINDOW, N_KV_HEADS, HEAD_DIM),
                sr.reshape(depth, b, N_SSM_GROUPS, SSM_STATE), si.reshape(depth, b, N_SSM_GROUPS, SSM_STATE),
                r.reshape(depth, b, N_RET_HEADS, HEAD_DIM, HEAD_DIM))

    return (hp, hs[:, :dec_seq]) + pack(st_p, bp) + pack(st_s, bs)
```

```python
import functools
import math

import numpy as np
import jax
import jax.numpy as jnp
from jax import lax
from jax.experimental import pallas as pl
from jax.experimental.pallas import tpu as pltpu

F32 = jnp.float32
BF16 = jnp.bfloat16

D_MODEL = 1024
HEAD_DIM = 64
D_ATTN = 512
N_Q_HEADS = 8
N_KV_HEADS = 2
Q_PER_KV = N_Q_HEADS // N_KV_HEADS
D_KV = 128
WINDOW = 128
ROPE_THETA = 500000.0
ROT_DIM = 16
D_SSM = 256
SSM_GROUP = 16
N_SSM_GROUPS = 16
SSM_STATE = 64
D_STATE = N_SSM_GROUPS * SSM_STATE
D_RET = 256
N_RET_HEADS = 4
RET_THETA = 10000.0
PLE_DIM = 256
EPS = 1e-6
LOG2E = math.log2(math.e)
D_IN = 2816
COL_A = 0
COL_B = 1280
COL_C = 1792
LANES = 128
MXU_N = 256
ROW_GROUP = 256
VMEM_LIMIT = 58 * 1024 * 1024
GROUPS_PROMPT = (2, 2)
GROUPS_SAMPLE = (8, 2)


def _mm(a, b):
    return jnp.dot(a.astype(BF16), b.astype(BF16), preferred_element_type=F32)


def _mm_nt(a, b):
    return lax.dot_general(a.astype(BF16), b.astype(BF16), (((1,), (1,)), ((), ())),
                           preferred_element_type=F32)


def _silu(x):
    return x * jax.nn.sigmoid(x)


def _layer_kernel(cfg,
                  h_ref, p_ref, rope_ref, dmask_ref, rdec_ref,
                  kwin_ref, vwin_ref, sre0_ref, sim0_ref, r0_ref,
                  sink_ref, ng_ref, win_ref, lam_ref, bbd_ref, wc_ref, d_ref, wglu_ref, rg_ref,
                  wout_ref, wpp_ref, wgate_ref, fg_ref,
                  hout_ref, kout_ref, vout_ref, sreo_ref, simo_ref, ro_ref,
                  kbuf, vbuf, sre, sim, rst, wb_s, a_s, hn_s, pa_s, pc_s, u_s, gb_s,
                  ya_s, yc_s, yb_s, ut_s, bu_s, h1_s, h1b_s, *resident):
    bblk, tc, tcp, n_chunks, pos0, layer0, n_layers, final_norm, att_group, ret_group = cfg
    w_ret = ret_group * HEAD_DIM
    n_rg = N_RET_HEADS // ret_group
    gsz = ROW_GROUP // tcp
    n_groups = bblk // gsz
    sb = min(tc, ROW_GROUP // bblk)
    assert tc // sb == n_groups
    assert tcp & (tcp - 1) == 0
    lyr = pl.program_id(0)
    n = pl.program_id(2)
    if resident:
        h_all, = resident
        hb = pl.multiple_of(pl.program_id(1) * bblk, bblk)

        @pl.when(lyr == 0)
        def _load_h():
            h_all[pl.ds(hb, bblk)] = h_ref[...]

        def h_rows(b0, cols=slice(None)):
            return h_all[pl.ds(hb + b0, gsz), :, cols]
    else:
        def h_rows(b0, cols=slice(None)):
            return h_ref[pl.ds(b0, gsz), :, cols]

    def head_of(shape, dim):
        return lax.broadcasted_iota(jnp.int32, shape, dim) >> (HEAD_DIM.bit_length() - 1)

    same_head = head_of((w_ret, w_ret), 0) == head_of((w_ret, w_ret), 1)

    @pl.when(n == 0)
    def _init():
        kbuf[:, 0:WINDOW] = kwin_ref[...]
        vbuf[:, 0:WINDOW] = vwin_ref[...]
        if tcp < WINDOW:
            kbuf[:, WINDOW:2 * WINDOW] = jnp.zeros((bblk, WINDOW, LANES), F32)
            vbuf[:, WINDOW:2 * WINDOW] = jnp.zeros((bblk, WINDOW, LANES), F32)
        sre[...] = sre0_ref[...]
        sim[...] = sim0_ref[...]
        r_in = r0_ref[...].reshape(bblk, n_rg, w_ret, HEAD_DIM)
        rst[...] = jnp.where(same_head, jnp.concatenate([r_in] * ret_group, axis=-1), 0.0)
        lr = lam_ref[0:1, :]
        li = lam_ref[1:2, :]
        dt = jnp.exp(lam_ref[2:3, :])
        mag = jnp.exp(lr * dt)
        a_re = mag * jnp.cos(li * dt)
        a_im = mag * jnp.sin(li * dt)
        nr = a_re - 1.0
        ni = a_im
        den = lr * lr + li * li
        coef_re = (nr * lr + ni * li) / den
        coef_im = (ni * lr - nr * li) / den
        a_s[0:1, :] = a_re
        a_s[1:2, :] = a_im
        br = bbd_ref[0]
        bi = bbd_ref[1]
        wb_s[:, 0:D_STATE] = (coef_re * br - coef_im * bi).astype(BF16)
        wb_s[:, D_STATE:2 * D_STATE] = (coef_re * bi + coef_im * br).astype(BF16)
        if tc < tcp:
            yb_s[...] = jnp.zeros_like(yb_s)

    lo_q = head_of((tcp, LANES), 1) == 0
    headmean = jnp.where(same_head, 1.0 / HEAD_DIM, 0.0).astype(BF16)
    headmean2 = jnp.concatenate([headmean, headmean], axis=0)
    ones_k = jnp.ones((2 * WINDOW, LANES), BF16)
    ret_lane_head = head_of((tcp, w_ret), 1)

    cos_a, sa_hi, sa_lo = rope_ref[0], rope_ref[1], rope_ref[2]
    cos_r, sr_hi, sr_lo = rope_ref[3], rope_ref[4], rope_ref[5]

    def rope_a(x):
        return (x * cos_a + pltpu.roll(x, ROT_DIM // 2, 1) * sa_hi
                + pltpu.roll(x, LANES - ROT_DIM // 2, 1) * sa_lo)

    def rope_r(x):
        return (x * cos_r + pltpu.roll(x, HEAD_DIM // 2, 1) * sr_hi
                + pltpu.roll(x, LANES - HEAD_DIM // 2, 1) * sr_lo)

    lmin = WINDOW - pos0 - n * tc

    def valid_keys(n_heads):
        qi = lax.broadcasted_iota(jnp.int32, (n_heads * tcp, 2 * WINDOW), 0) & (tcp - 1)
        kl = lax.broadcasted_iota(jnp.int32, (n_heads * tcp, 2 * WINDOW), 1)
        return (kl > qi) & (kl <= qi + WINDOW) & (kl >= lmin)

    valid = valid_keys(1)
    valid2 = valid_keys(2)
    first_head = lax.broadcasted_iota(jnp.int32, (2 * tcp, 1), 0) < tcp
    lo_k = head_of((2 * WINDOW, LANES), 1) == 0

    def norm_body(g, carry):
        b0 = pl.multiple_of(g * gsz, gsz)
        r0 = pl.multiple_of(g * ROW_GROUP, ROW_GROUP)
        x = h_rows(b0).reshape(ROW_GROUP, D_MODEL)
        ms = jnp.mean(x * x, axis=-1, keepdims=True)
        hn = (x * lax.rsqrt(ms + EPS) * ng_ref[...]).astype(BF16)
        hn_s[pl.ds(r0, ROW_GROUP), :] = hn
        ub = jnp.dot(hn, win_ref[:, COL_B:COL_B + D_SSM], preferred_element_type=F32)
        u_s[0, pl.ds(r0, ROW_GROUP), :] = ub[:, 0:LANES]
        u_s[1, pl.ds(r0, ROW_GROUP), :] = ub[:, LANES:2 * LANES]
        gb_s[pl.ds(r0, ROW_GROUP), :] = jnp.dot(hn, win_ref[:, COL_B + D_SSM:COL_C],
                                                preferred_element_type=F32)
        return carry

    lax.fori_loop(0, n_groups, norm_body, 0)

    a_re_b = jnp.broadcast_to(a_s[0:1, :], (bblk, D_STATE))
    a_im_b = jnp.broadcast_to(a_s[1:2, :], (bblk, D_STATE))

    def s5_sub_block(si):
        t0 = si * sb
        for t in range(sb):
            for s in range(2):
                ut_s[t * bblk:(t + 1) * bblk, s * LANES:(s + 1) * LANES] = (
                    u_s[s, pl.ds(t0 + t, bblk, stride=tcp), :])
        for c in range(0, 2 * D_STATE, MXU_N):
            bu_s[:, c:c + MXU_N] = _mm(ut_s[...], wb_s[:, c:c + MXU_N])

        def step(t, xs):
            xr, xi = xs
            rt = pl.multiple_of(t * bblk, bblk)
            b_re = bu_s[pl.ds(rt, bblk), 0:D_STATE]
            b_im = bu_s[pl.ds(rt, bblk), D_STATE:2 * D_STATE]
            nxr = a_re_b * xr - a_im_b * xi + b_re
            nxi = a_re_b * xi + a_im_b * xr + b_im
            bu_s[pl.ds(rt, bblk), 0:D_STATE] = nxr
            bu_s[pl.ds(rt, bblk), D_STATE:2 * D_STATE] = nxi
            return nxr, nxi

        xr, xi = lax.fori_loop(0, sb, step, (sre[...], sim[...]), unroll=True)
        sre[...] = xr
        sim[...] = xi
        y = (_mm(bu_s[:, 0:D_STATE], wc_ref[0]) - _mm(bu_s[:, D_STATE:2 * D_STATE], wc_ref[1])
             + d_ref[...] * ut_s[...])
        y = jax.nn.gelu(y)
        y = y * jax.nn.sigmoid(_mm(y, wglu_ref[...]))
        for t in range(sb):
            for s in range(2):
                yb_s[s, pl.ds(t0 + t, bblk, stride=tcp), :] = y[t * bblk:(t + 1) * bblk, s * LANES:(s + 1) * LANES]

    def group_body(g, carry):
        b0 = pl.multiple_of(g * gsz, gsz)
        r0 = pl.multiple_of(g * ROW_GROUP, ROW_GROUP)
        hn = hn_s[pl.ds(r0, ROW_GROUP), :]
        for c in range(0, COL_B - COL_A, MXU_N):
            pa_s[:, c:c + MXU_N] = jnp.dot(hn, win_ref[:, COL_A + c:COL_A + c + MXU_N],
                                           preferred_element_type=F32)
        for c in range(0, D_IN - COL_C, MXU_N):
            pc_s[:, c:c + MXU_N] = jnp.dot(hn, win_ref[:, COL_C + c:COL_C + c + MXU_N],
                                           preferred_element_type=F32)

        def batch_body(j, carry2):
            b = b0 + j
            rj = pl.multiple_of(j * tcp, tcp)
            rb = pl.multiple_of(b * tcp, tcp)
            k_cur = rope_a(pa_s[pl.ds(rj, tcp), D_ATTN:D_ATTN + D_KV])
            v_cur = pa_s[pl.ds(rj, tcp), D_ATTN + D_KV:D_ATTN + 2 * D_KV]
            if tcp == WINDOW:
                kout_ref[b] = k_cur
                vout_ref[b] = v_cur
            else:
                keep = lax.broadcasted_iota(jnp.int32, (tcp, LANES), 0) < tcp - tc
                for buf, cur, out in ((kbuf, k_cur, kout_ref), (vbuf, v_cur, vout_ref)):
                    shifted = pltpu.roll(buf[b, 0:WINDOW], WINDOW - tc, 0)
                    out[b, 0:WINDOW - tcp] = shifted[0:WINDOW - tcp]
                    out[b, WINDOW - tcp:WINDOW] = jnp.where(keep, shifted[WINDOW - tcp:WINDOW],
                                                            pltpu.roll(cur, tcp - tc, 0))
            kbuf[b, WINDOW:WINDOW + tcp] = k_cur
            vbuf[b, WINDOW:WINDOW + tcp] = v_cur
            def q_pair(pj):
                return rope_a(pa_s[pl.ds(rj, tcp), pj * LANES:(pj + 1) * LANES])

            def softmax_matmul(q_grp, heads, k_op, v_op):
                s_grp = _mm_nt(q_grp * (HEAD_DIM ** -0.5 * LOG2E), k_op)
                ex, sink_w = [], []
                for t, hd in enumerate(heads):
                    s = jnp.where(valid, s_grp[t * tcp:(t + 1) * tcp], -jnp.inf)
                    sk = sink_ref[layer0 + lyr, hd] * LOG2E
                    m = jnp.maximum(jnp.max(s, axis=-1, keepdims=True), sk)
                    ex.append(jnp.exp2(s - m).astype(BF16))
                    sink_w.append(jnp.exp2(sk - m))
                r = _mm(jnp.concatenate(ex, axis=0), jnp.concatenate([v_op.astype(BF16), ones_k], axis=1))
                o_grp = r[:, 0:LANES] / (r[:, LANES:2 * LANES] + jnp.concatenate(sink_w, axis=0))
                return [o_grp[t * tcp:(t + 1) * tcp] for t in range(len(heads))]

            k_cat, v_cat = kbuf[b], vbuf[b]
            hi_q = jnp.logical_not(lo_q)
            o_heads = [None] * N_Q_HEADS
            if att_group == N_Q_HEADS:
                q_rows = []
                for pj in range(N_Q_HEADS // 2):
                    qp = q_pair(pj)
                    qx = pltpu.roll(qp, HEAD_DIM, 1)
                    kv_lo = (pj // 2 == 0)
                    q_even, q_odd = (qp, qx) if kv_lo else (qx, qp)
                    keep = lo_q if kv_lo else hi_q
                    q_rows += [jnp.where(keep, q_even, 0.0), jnp.where(keep, q_odd, 0.0)]
                o_heads = softmax_matmul(jnp.concatenate(q_rows, axis=0), range(N_Q_HEADS), k_cat, v_cat)
                for hd in range(N_Q_HEADS):
                    if (hd % 2 == 0) != (hd // Q_PER_KV == 0):
                        o_heads[hd] = pltpu.roll(o_heads[hd], HEAD_DIM, 1)
            else:
                k_rol, v_rol = pltpu.roll(k_cat, HEAD_DIM, 1), pltpu.roll(v_cat, HEAD_DIM, 1)
                for kv in range(N_KV_HEADS):
                    qs = jnp.concatenate([q_pair(2 * kv), q_pair(2 * kv + 1)], axis=0)
                    qs = (qs * (HEAD_DIM ** -0.5 * LOG2E)).astype(BF16)
                    o = jnp.zeros((2 * tcp, LANES), F32)
                    for e in range(2):
                        keep = lo_k if e == 0 else jnp.logical_not(lo_k)
                        k_op = jnp.where(keep, k_cat if e == kv else k_rol, 0.0)
                        v_op = jnp.where(keep, v_cat if e == kv else v_rol, 0.0)
                        s = jnp.where(valid2, _mm_nt(qs, k_op), -jnp.inf)
                        sk = jnp.where(first_head, sink_ref[layer0 + lyr, 4 * kv + e],
                                       sink_ref[layer0 + lyr, 4 * kv + 2 + e]) * LOG2E
                        m = jnp.maximum(jnp.max(s, axis=-1, keepdims=True), sk)
                        r = _mm(jnp.exp2(s - m), jnp.concatenate([v_op.astype(BF16), ones_k], axis=1))
                        o = o + r[:, 0:LANES] / (r[:, LANES:2 * LANES] + jnp.exp2(sk - m))
                    o_heads[4 * kv:4 * kv + 4] = [o[0:tcp], o[0:tcp], o[tcp:2 * tcp], o[tcp:2 * tcp]]
            for pj in range(N_Q_HEADS // 2):
                ga = pa_s[pl.ds(rj, tcp),
                          D_ATTN + 2 * D_KV + pj * LANES:D_ATTN + 2 * D_KV + (pj + 1) * LANES]
                ya_s[pl.ds(rb, tcp), pj * LANES:(pj + 1) * LANES] = (
                    jnp.where(lo_q, o_heads[2 * pj], o_heads[2 * pj + 1]) * _silu(ga))
            if n_chunks > 1:
                kbuf[b, 0:WINDOW] = k_cur
                vbuf[b, 0:WINDOW] = v_cur
            for pr in range(n_rg):
                cs = slice(pr * w_ret, (pr + 1) * w_ret)

                def rope_cols(c0):
                    return jnp.concatenate([rope_r(pc_s[pl.ds(rj, tcp), c0 + t * LANES:c0 + (t + 1) * LANES])
                                            for t in range(w_ret // LANES)], axis=1)

                q = rope_cols(pr * w_ret)
                k = rope_cols(D_RET + pr * w_ret) * (HEAD_DIM ** -0.5)
                v = pc_s[pl.ds(rj, tcp), 2 * D_RET + pr * w_ret:2 * D_RET + (pr + 1) * w_ret]
                gr = pc_s[pl.ds(rj, tcp), 3 * D_RET + pr * w_ret:3 * D_RET + (pr + 1) * w_ret]
                r_old = rst[b, pr]
                k_heads = jnp.concatenate([jnp.where(ret_lane_head == hd, k, 0.0) for hd in range(ret_group)], axis=0)
                v_heads = jnp.concatenate([jnp.where(ret_lane_head == hd, v, 0.0) for hd in range(ret_group)], axis=0)
                sc = _mm_nt(q, k_heads) * dmask_ref[pr]
                o = _mm(q, r_old) * rdec_ref[1, :, cs] + _mm(sc, v_heads)
                kd = k * rdec_ref[0, :, cs]
                if tcp < WINDOW:
                    pad = jnp.zeros((WINDOW - tcp, w_ret), F32)
                    d_new = _mm(jnp.concatenate([kd, pad], axis=0).T, jnp.concatenate([v, pad], axis=0))
                else:
                    d_new = _mm(kd.T, v)
                rst[b, pr] = r_old * rdec_ref[2, 0:1, cs] + jnp.where(same_head, d_new, 0.0)
                o2 = o * o
                o2_hi = o2.astype(BF16)
                o2_lo = (o2 - o2_hi.astype(F32)).astype(BF16)
                msq = jnp.dot(jnp.concatenate([o2_hi, o2_lo], axis=1), headmean2, preferred_element_type=F32)
                yc_s[pl.ds(rb, tcp), cs] = o * lax.rsqrt(msq + EPS) * rg_ref[:, cs] * _silu(gr)
            return carry2

        lax.fori_loop(0, gsz, batch_body, 0, unroll=min(gsz, 8))
        s5_sub_block(g)
        return carry

    lax.fori_loop(0, n_groups, group_body, 0, unroll=min(n_groups, 2))

    def out_body(g, carry):
        b0 = pl.multiple_of(g * gsz, gsz)
        r0 = pl.multiple_of(g * ROW_GROUP, ROW_GROUP)
        ya = ya_s[pl.ds(r0, ROW_GROUP), :].astype(BF16)
        gate_b = _silu(gb_s[pl.ds(r0, ROW_GROUP), :])
        yb0 = (yb_s[0, pl.ds(r0, ROW_GROUP), :] * gate_b[:, 0:LANES]).astype(BF16)
        yb1 = (yb_s[1, pl.ds(r0, ROW_GROUP), :] * gate_b[:, LANES:2 * LANES]).astype(BF16)
        yc = yc_s[pl.ds(r0, ROW_GROUP), :].astype(BF16)
        for c in range(0, D_MODEL, MXU_N):
            cc = slice(c, c + MXU_N)
            y = (jnp.dot(ya, wout_ref[0:D_ATTN, cc], preferred_element_type=F32)
                 + jnp.dot(yb0, wout_ref[D_ATTN:D_ATTN + LANES, cc], preferred_element_type=F32)
                 + jnp.dot(yb1, wout_ref[D_ATTN + LANES:D_ATTN + D_SSM, cc], preferred_element_type=F32)
                 + jnp.dot(yc, wout_ref[D_ATTN + D_SSM:D_MODEL, cc], preferred_element_type=F32))
            h1 = h_rows(b0, cc).reshape(ROW_GROUP, MXU_N) + y
            h1_s[:, cc] = h1
            h1b_s[:, cc] = h1.astype(BF16)
        pb = p_ref[pl.ds(b0, gsz)].reshape(ROW_GROUP, PLE_DIM).astype(BF16)
        ssq = jnp.zeros((ROW_GROUP, 1), F32)
        for c in range(0, D_MODEL, MXU_N):
            cc = slice(c, c + MXU_N)
            gate = jax.nn.sigmoid(jnp.dot(h1b_s[...], wgate_ref[:, cc], preferred_element_type=F32))
            pp = jnp.dot(pb, wpp_ref[:, cc], preferred_element_type=F32)
            h2 = h1_s[:, cc] + gate * pp
            if final_norm:
                ssq = ssq + jnp.sum(h2 * h2, axis=-1, keepdims=True)
                h1_s[:, cc] = h2
            else:
                hout_ref[pl.ds(b0, gsz), :, cc] = h2.reshape(gsz, tcp, MXU_N)

        def write_normed():
            scale = lax.rsqrt(ssq * (1.0 / D_MODEL) + EPS)
            hout_ref[pl.ds(b0, gsz)] = (h1_s[...] * scale * fg_ref[...]).reshape(gsz, tcp, D_MODEL)

        if resident:
            h_all[pl.ds(hb + b0, gsz)] = h1_s[...].reshape(gsz, tcp, D_MODEL)
            pl.when(lyr == n_layers - 1)(write_normed)
        elif final_norm:
            write_normed()
        return carry

    lax.fori_loop(0, n_groups, out_body, 0)

    @pl.when(n == n_chunks - 1)
    def _finish():
        sreo_ref[...] = sre[...]
        simo_ref[...] = sim[...]
        r_fin = rst[...]
        ro_ref[...] = sum(r_fin[..., hd * HEAD_DIM:(hd + 1) * HEAD_DIM] for hd in range(ret_group)).reshape(
            bblk, N_RET_HEADS * HEAD_DIM, HEAD_DIM)


def _rope_tables(pos0, t_pad):
    pos = jnp.arange(t_pad, dtype=F32) + pos0
    lane = np.arange(LANES) % HEAD_DIM

    def tables(theta, rot_dim):
        half = rot_dim // 2
        inv = 1.0 / jnp.power(theta, jnp.arange(half, dtype=F32) / half)
        ang = pos[:, None] * inv[None, :]
        cos = jnp.cos(ang)
        sin = jnp.sin(ang)
        idx = np.where(lane < rot_dim, lane % half, 0)
        in_rot = jnp.asarray(lane < rot_dim)
        is_hi = jnp.asarray((lane >= half) & (lane < rot_dim))
        is_lo = jnp.asarray(lane < half)
        cos_l = jnp.where(in_rot[None, :], cos[:, idx], 1.0)
        sin_l = sin[:, idx]
        return (cos_l, jnp.where(is_hi[None, :], sin_l, 0.0), jnp.where(is_lo[None, :], -sin_l, 0.0))

    return jnp.stack(tables(ROPE_THETA, ROT_DIM) + tables(RET_THETA, HEAD_DIM))


def _retention_tables(tc, tcp, ret_group):
    gammas = 1.0 - jnp.power(2.0, -5.0 - jnp.arange(N_RET_HEADS, dtype=F32))
    log_g = jnp.log(gammas)
    i = jnp.arange(tcp, dtype=F32)
    rel = i[:, None] - i[None, :]
    dmask = jnp.where(rel[None] >= 0, jnp.exp(log_g[:, None, None] * jnp.maximum(rel, 0.0)[None]), 0.0)
    kdec = jnp.exp(log_g[None, :] * (tc - 1.0 - i)[:, None])
    qdec = jnp.exp(log_g[None, :] * (i + 1.0)[:, None])
    gch = jnp.broadcast_to(jnp.exp(log_g * tc)[None, :], (tcp, N_RET_HEADS))
    rdec = jnp.stack([jnp.repeat(t, HEAD_DIM, axis=1) for t in (kdec, qdec, gch)])
    dmask = dmask.reshape(N_RET_HEADS // ret_group, ret_group, tcp, tcp).transpose(0, 2, 1, 3).reshape(
        N_RET_HEADS // ret_group, tcp, ret_group * tcp)
    return dmask, rdec


def _layer_call(cfg, batch, h, p_all, rope, dmask, rdec, kwin, vwin, sre0, sim0, r0, weights):
    bblk, tc, tcp, n_chunks, pos0, layer0, n_layers, final_norm, att_group, ret_group = cfg
    w_ret = ret_group * HEAD_DIM
    resident_h = n_layers > 1
    assert not resident_h or (n_chunks == 1 and final_norm)
    rows = bblk * tcp
    rows_sb = min(tc, ROW_GROUP // bblk) * bblk
    nb = batch // bblk
    t_pad = n_chunks * tcp
    const = pl.Buffered(1)

    def wspec(shape):
        nd = len(shape)
        return pl.BlockSpec((None,) + tuple(shape[1:]), lambda l, bb, n: (layer0 + l,) + (0,) * (nd - 1),
                            pipeline_mode=const)

    def sspec(shape, lead_layer, mode=None):
        nd = len(shape)
        if lead_layer:
            return pl.BlockSpec((None, bblk) + tuple(shape[2:]),
                                lambda l, bb, n: (layer0 + l, bb) + (0,) * (nd - 2), pipeline_mode=mode)
        return pl.BlockSpec((bblk,) + tuple(shape[1:]), lambda l, bb, n: (bb,) + (0,) * (nd - 1),
                            pipeline_mode=mode)

    big_in_mode = const if resident_h else None

    def ospec(tail):
        return pl.BlockSpec((None, bblk) + tail, lambda l, bb, n: (l, bb) + (0,) * len(tail))

    if resident_h:
        h_in_map = lambda l, bb, n: (jnp.where(l == 0, bb, 0), 0, 0)
        h_out_map = lambda l, bb, n: (jnp.where(l == n_layers - 1, bb, 0), 0, 0)
    else:
        h_in_map = h_out_map = lambda l, bb, n: (bb, n, 0)

    (sink, ng, w_in, lam, bbd, wc, d_row, wglu, rg, w_out, w_pp, w_gate, fg) = weights
    in_specs = [
        pl.BlockSpec((bblk, tcp, D_MODEL), h_in_map, pipeline_mode=big_in_mode),
        pl.BlockSpec((None, bblk, tcp, PLE_DIM), lambda l, bb, n: (layer0 + l, bb, n, 0)),
        pl.BlockSpec((6, tcp, LANES), lambda l, bb, n: (0, n, 0)),
        pl.BlockSpec(dmask.shape, lambda l, bb, n: (0, 0, 0), pipeline_mode=const),
        pl.BlockSpec(rdec.shape, lambda l, bb, n: (0, 0, 0), pipeline_mode=const),
        sspec(kwin.shape, kwin.ndim == 4), sspec(vwin.shape, vwin.ndim == 4),
        sspec(sre0.shape, sre0.ndim == 3), sspec(sim0.shape, sim0.ndim == 3),
        sspec(r0.shape, r0.ndim == 4),
        pl.BlockSpec(memory_space=pltpu.SMEM),
        wspec(ng.shape), wspec(w_in.shape), wspec(lam.shape), wspec(bbd.shape), wspec(wc.shape),
        wspec(d_row.shape), wspec(wglu.shape), wspec(rg.shape), wspec(w_out.shape), wspec(w_pp.shape),
        wspec(w_gate.shape),
        pl.BlockSpec(fg.shape, lambda l, bb, n: (0, 0), pipeline_mode=const),
    ]
    out_shape = (
        jax.ShapeDtypeStruct((batch, t_pad, D_MODEL), F32),
        jax.ShapeDtypeStruct((n_layers, batch, WINDOW, LANES), F32),
        jax.ShapeDtypeStruct((n_layers, batch, WINDOW, LANES), F32),
        jax.ShapeDtypeStruct((n_layers, batch, D_STATE), F32),
        jax.ShapeDtypeStruct((n_layers, batch, D_STATE), F32),
        jax.ShapeDtypeStruct((n_layers, batch, N_RET_HEADS * HEAD_DIM, HEAD_DIM), F32),
    )
    out_specs = (
        pl.BlockSpec((bblk, tcp, D_MODEL), h_out_map),
        ospec((WINDOW, LANES)), ospec((WINDOW, LANES)), ospec((D_STATE,)), ospec((D_STATE,)),
        ospec((N_RET_HEADS * HEAD_DIM, HEAD_DIM)),
    )
    scratch = [
        pltpu.VMEM((bblk, 2 * WINDOW, LANES), F32),
        pltpu.VMEM((bblk, 2 * WINDOW, LANES), F32),
        pltpu.VMEM((bblk, D_STATE), F32),
        pltpu.VMEM((bblk, D_STATE), F32),
        pltpu.VMEM((bblk, N_RET_HEADS // ret_group, w_ret, w_ret), F32),
        pltpu.VMEM((D_SSM, 2 * D_STATE), BF16),
        pltpu.VMEM((2, D_STATE), F32),
        pltpu.VMEM((rows, D_MODEL), BF16),
        pltpu.VMEM((ROW_GROUP, COL_B - COL_A), F32),
        pltpu.VMEM((ROW_GROUP, D_IN - COL_C), F32),
        pltpu.VMEM((2, rows, LANES), F32),
        pltpu.VMEM((rows, D_SSM), F32),
        pltpu.VMEM((rows, D_ATTN), F32),
        pltpu.VMEM((rows, D_RET), F32),
        pltpu.VMEM((2, rows, LANES), F32),
        pltpu.VMEM((rows_sb, D_SSM), F32),
        pltpu.VMEM((rows_sb, 2 * D_STATE), F32),
        pltpu.VMEM((ROW_GROUP, D_MODEL), F32),
        pltpu.VMEM((ROW_GROUP, D_MODEL), BF16),
    ]
    if resident_h:
        scratch.append(pltpu.VMEM((batch, tcp, D_MODEL), F32))
    return pl.pallas_call(
        functools.partial(_layer_kernel, cfg),
        grid=(n_layers, nb, n_chunks),
        in_specs=in_specs,
        out_specs=out_specs,
        out_shape=out_shape,
        scratch_shapes=scratch,
        compiler_params=pltpu.CompilerParams(
            dimension_semantics=("arbitrary", "arbitrary", "arbitrary"),
            vmem_limit_bytes=VMEM_LIMIT),
        name="sample_layers" if resident_h else "prompt_layer_%d" % layer0,
    )(h, p_all, rope, dmask, rdec, kwin, vwin, sre0, sim0, r0,
      sink, ng, w_in, lam, bbd, wc, d_row, wglu, rg, w_out, w_pp, w_gate, fg)


def kernel(x_prompt, x_sample, p_prompt, p_sample, cache_win_k, cache_win_v, state_ssm_re, state_ssm_im,
           state_ret, norm_g, w_in, attn_sink, ssm_lam_re, ssm_lam_im, ssm_log_dt, ssm_b_re, ssm_b_im,
           ssm_c_re, ssm_c_im, ssm_d, w_glu, ret_norm_g, w_out, w_ple_proj, w_ple_gate, final_norm_g):
    depth = w_in.shape[0]
    bp, seq = x_prompt.shape[0], x_prompt.shape[1]
    bs, dec_seq = x_sample.shape[0], x_sample.shape[1]
    past_len = 8192

    same_group = jnp.asarray(np.arange(D_SSM)[:, None] // SSM_GROUP == np.arange(D_STATE)[None, :] // SSM_STATE)
    bbd = jnp.stack([
        jnp.where(same_group, jnp.tile(b.transpose(0, 1, 3, 2).reshape(depth, D_SSM, SSM_STATE),
                                       (1, 1, N_SSM_GROUPS)), 0.0)
        for b in (ssm_b_re, ssm_b_im)], axis=1)
    wc = jnp.stack([
        jnp.where(same_group.T, jnp.tile(c.transpose(0, 1, 3, 2).reshape(depth, D_STATE, SSM_GROUP),
                                         (1, 1, N_SSM_GROUPS)), 0.0).astype(BF16)
        for c in (ssm_c_re, ssm_c_im)], axis=1)
    lam = jnp.stack([ssm_lam_re.reshape(depth, D_STATE), ssm_lam_im.reshape(depth, D_STATE),
                     jnp.repeat(ssm_log_dt, SSM_STATE, axis=-1)], axis=1)
    weights = (
        attn_sink,
        norm_g.reshape(depth, 1, D_MODEL), w_in.astype(BF16), lam, bbd, wc,
        ssm_d.reshape(depth, 1, D_SSM), w_glu.astype(BF16), ret_norm_g.reshape(depth, 1, D_RET),
        w_out.astype(BF16), w_ple_proj.astype(BF16), w_ple_gate.astype(BF16),
        final_norm_g.reshape(1, D_MODEL),
    )

    tc_p = WINDOW
    rope_p = _rope_tables(0.0, seq)
    dmask_p, rdec_p = _retention_tables(tc_p, tc_p, GROUPS_PROMPT[1])
    zk = jnp.zeros((bp, WINDOW, LANES), F32)
    zs = jnp.zeros((bp, D_STATE), F32)
    zr = jnp.zeros((bp, N_RET_HEADS * HEAD_DIM, HEAD_DIM), F32)

    tcp_s = 16
    bblk_s = ROW_GROUP // tcp_s
    rope_s = _rope_tables(float(past_len), tcp_s)
    dmask_s, rdec_s = _retention_tables(dec_seq, tcp_s, GROUPS_SAMPLE[1])
    pad_t = ((0, 0), (0, tcp_s - dec_seq), (0, 0))
    hs = jnp.pad(x_sample, pad_t)
    ps = jnp.pad(p_sample, ((0, 0),) + pad_t)
    kwin_s = cache_win_k.reshape(depth, bs, WINDOW, LANES)
    vwin_s = cache_win_v.reshape(depth, bs, WINDOW, LANES)
    sre_s = state_ssm_re.reshape(depth, bs, D_STATE)
    sim_s = state_ssm_im.reshape(depth, bs, D_STATE)
    r_s = state_ret.reshape(depth, bs, N_RET_HEADS * HEAD_DIM, HEAD_DIM)

    hp = x_prompt
    outs_p = []
    for i in range(depth):
        cfg_p = (bp, tc_p, tc_p, seq // tc_p, 0, i, 1, i == depth - 1) + GROUPS_PROMPT
        hp, *st = _layer_call(cfg_p, bp, hp, p_prompt, rope_p, dmask_p, rdec_p, zk, zk, zs, zs, zr, weights)
        outs_p.append(st)
    st_p = [jnp.concatenate(t, axis=0) for t in zip(*outs_p)]
    cfg_s = (bblk_s, dec_seq, tcp_s, 1, past_len, 0, depth, True) + GROUPS_SAMPLE
    hs, *st_s = _layer_call(cfg_s, bs, hs, ps, rope_s, dmask_s, rdec_s, kwin_s, vwin_s, sre_s, sim_s, r_s, weights)

    def pack(st, b):
        k, v, sr, si, r = st
        return (k.reshape(depth, b, WINDOW, N_KV_HEADS, HEAD_DIM), v.reshape(depth, b, WINDOW, N_KV_HEADS, HEAD_DIM),
                sr.reshape(depth, b, N_SSM_GROUPS, SSM_STATE), si.reshape(depth, b, N_SSM_GROUPS, SSM_STATE),
                r.reshape(depth, b, N_RET_HEADS, HEAD_DIM, HEAD_DIM))

    return (hp, hs[:, :dec_seq]) + pack(st_p, bp) + pack(st_s, bs)
```

```python
import functools
import math

import numpy as np
import jax
import jax.numpy as jnp
from jax import lax
from jax.experimental import pallas as pl
from jax.experimental.pallas import tpu as pltpu

F32 = jnp.float32
BF16 = jnp.bfloat16

D_MODEL = 1024
HEAD_DIM = 64
D_ATTN = 512
N_Q_HEADS = 8
N_KV_HEADS = 2
Q_PER_KV = N_Q_HEADS // N_KV_HEADS
D_KV = 128
WINDOW = 128
ROPE_THETA = 500000.0
ROT_DIM = 16
D_SSM = 256
SSM_GROUP = 16
N_SSM_GROUPS = 16
SSM_STATE = 64
D_STATE = N_SSM_GROUPS * SSM_STATE
D_RET = 256
N_RET_HEADS = 4
RET_THETA = 10000.0
PLE_DIM = 256
EPS = 1e-6
LOG2E = math.log2(math.e)
D_IN = 2816
COL_A = 0
COL_B = 1280
COL_C = 1792
LANES = 128
MXU_N = 256
ROW_GROUP = 256
VMEM_LIMIT = 58 * 1024 * 1024
GROUPS_PROMPT = (2, 2)
GROUPS_SAMPLE = (8, 2)


def _mm(a, b):
    return jnp.dot(a.astype(BF16), b.astype(BF16), preferred_element_type=F32)


def _mm_nt(a, b):
    return lax.dot_general(a.astype(BF16), b.astype(BF16), (((1,), (1,)), ((), ())),
                           preferred_element_type=F32)


def _silu(x):
    return x * jax.nn.sigmoid(x)


def _layer_kernel(cfg,
                  h_ref, p_ref, rope_ref, dmask_ref, rdec_ref,
                  kwin_ref, vwin_ref, sre0_ref, sim0_ref, r0_ref,
                  sink_ref, ng_ref, win_ref, lam_ref, bbd_ref, wc_ref, d_ref, wglu_ref, rg_ref,
                  wout_ref, wpp_ref, wgate_ref, fg_ref,
                  hout_ref, kout_ref, vout_ref, sreo_ref, simo_ref, ro_ref,
                  kbuf, vbuf, sre, sim, rst, wb_s, a_s, hn_s, pa_s, pc_s, u_s, gb_s,
                  ya_s, yc_s, yb_s, ut_s, bu_s, h1_s, h1b_s, *resident):
    bblk, tc, tcp, n_chunks, pos0, layer0, n_layers, final_norm, att_group, ret_group = cfg
    w_ret = ret_group * HEAD_DIM
    n_rg = N_RET_HEADS // ret_group
    rgrp = min(ROW_GROUP, bblk * tcp)
    gsz = rgrp // tcp
    n_groups = bblk // gsz
    sb = min(tc, rgrp // bblk)
    assert tc // sb == n_groups
    assert tcp & (tcp - 1) == 0
    lyr = pl.program_id(0)
    n = pl.program_id(2)
    if resident:
        h_all, = resident
        hb = pl.multiple_of(pl.program_id(1) * bblk, bblk)

        @pl.when(lyr == 0)
        def _load_h():
            h_all[pl.ds(hb, bblk)] = h_ref[...]

        def h_rows(b0, cols=slice(None)):
            return h_all[pl.ds(hb + b0, gsz), :, cols]
    else:
        def h_rows(b0, cols=slice(None)):
            return h_ref[pl.ds(b0, gsz), :, cols]

    def head_of(shape, dim):
        return lax.broadcasted_iota(jnp.int32, shape, dim) >> (HEAD_DIM.bit_length() - 1)

    same_head = head_of((w_ret, w_ret), 0) == head_of((w_ret, w_ret), 1)

    @pl.when(n == 0)
    def _init():
        kbuf[:, 0:WINDOW] = kwin_ref[...]
        vbuf[:, 0:WINDOW] = vwin_ref[...]
        if tcp < WINDOW:
            kbuf[:, WINDOW:2 * WINDOW] = jnp.zeros((bblk, WINDOW, LANES), F32)
            vbuf[:, WINDOW:2 * WINDOW] = jnp.zeros((bblk, WINDOW, LANES), F32)
        sre[...] = sre0_ref[...]
        sim[...] = sim0_ref[...]
        r_in = r0_ref[...].reshape(bblk, n_rg, w_ret, HEAD_DIM)
        rst[...] = jnp.where(same_head, jnp.concatenate([r_in] * ret_group, axis=-1), 0.0)
        if tc < tcp:
            yb_s[...] = jnp.zeros_like(yb_s)

    @pl.when((n == 0) & (pl.program_id(1) == 0))
    def _discretise():
        lr = lam_ref[0:1, :]
        li = lam_ref[1:2, :]
        dt = jnp.exp(lam_ref[2:3, :])
        mag = jnp.exp(lr * dt)
        a_re = mag * jnp.cos(li * dt)
        a_im = mag * jnp.sin(li * dt)
        nr = a_re - 1.0
        ni = a_im
        den = lr * lr + li * li
        coef_re = (nr * lr + ni * li) / den
        coef_im = (ni * lr - nr * li) / den
        a_s[0:1, :] = a_re
        a_s[1:2, :] = a_im
        br = bbd_ref[0]
        bi = bbd_ref[1]
        wb_s[:, 0:D_STATE] = (coef_re * br - coef_im * bi).astype(BF16)
        wb_s[:, D_STATE:2 * D_STATE] = (coef_re * bi + coef_im * br).astype(BF16)

    lo_q = head_of((tcp, LANES), 1) == 0
    headmean = jnp.where(same_head, 1.0 / HEAD_DIM, 0.0).astype(BF16)
    headmean2 = jnp.concatenate([headmean, headmean], axis=0)
    ones_k = jnp.ones((2 * WINDOW, LANES), BF16)
    ret_lane_head = head_of((tcp, w_ret), 1)

    cos_a, sa_hi, sa_lo = rope_ref[0], rope_ref[1], rope_ref[2]
    cos_r, sr_hi, sr_lo = rope_ref[3], rope_ref[4], rope_ref[5]

    def rope_a(x):
        return (x * cos_a + pltpu.roll(x, ROT_DIM // 2, 1) * sa_hi
                + pltpu.roll(x, LANES - ROT_DIM // 2, 1) * sa_lo)

    def rope_r(x):
        return (x * cos_r + pltpu.roll(x, HEAD_DIM // 2, 1) * sr_hi
                + pltpu.roll(x, LANES - HEAD_DIM // 2, 1) * sr_lo)

    lmin = WINDOW - pos0 - n * tc

    def valid_keys(n_heads):
        qi = lax.broadcasted_iota(jnp.int32, (n_heads * tcp, 2 * WINDOW), 0) & (tcp - 1)
        kl = lax.broadcasted_iota(jnp.int32, (n_heads * tcp, 2 * WINDOW), 1)
        return (kl > qi) & (kl <= qi + WINDOW) & (kl >= lmin)

    valid = valid_keys(1)
    valid2 = valid_keys(2)
    first_head = lax.broadcasted_iota(jnp.int32, (2 * tcp, 1), 0) < tcp
    lo_k = head_of((2 * WINDOW, LANES), 1) == 0

    def norm_body(g, carry):
        b0 = pl.multiple_of(g * gsz, gsz)
        r0 = pl.multiple_of(g * rgrp, rgrp)
        x = h_rows(b0).reshape(rgrp, D_MODEL)
        ms = jnp.mean(x * x, axis=-1, keepdims=True)
        hn = (x * lax.rsqrt(ms + EPS) * ng_ref[...]).astype(BF16)
        hn_s[pl.ds(r0, rgrp), :] = hn
        ub = jnp.dot(hn, win_ref[:, COL_B:COL_B + D_SSM], preferred_element_type=F32)
        u_s[0, pl.ds(r0, rgrp), :] = ub[:, 0:LANES]
        u_s[1, pl.ds(r0, rgrp), :] = ub[:, LANES:2 * LANES]
        gb_s[pl.ds(r0, rgrp), :] = jnp.dot(hn, win_ref[:, COL_B + D_SSM:COL_C],
                                                preferred_element_type=F32)
        return carry

    lax.fori_loop(0, n_groups, norm_body, 0)

    a_re_b = jnp.broadcast_to(a_s[0:1, :], (bblk, D_STATE))
    a_im_b = jnp.broadcast_to(a_s[1:2, :], (bblk, D_STATE))

    def s5_sub_block(si):
        t0 = si * sb
        for t in range(sb):
            for s in range(2):
                ut_s[t * bblk:(t + 1) * bblk, s * LANES:(s + 1) * LANES] = (
                    u_s[s, pl.ds(t0 + t, bblk, stride=tcp), :])
        for c in range(0, 2 * D_STATE, MXU_N):
            bu_s[:, c:c + MXU_N] = _mm(ut_s[...], wb_s[:, c:c + MXU_N])

        def step(t, xs):
            xr, xi = xs
            rt = pl.multiple_of(t * bblk, bblk)
            b_re = bu_s[pl.ds(rt, bblk), 0:D_STATE]
            b_im = bu_s[pl.ds(rt, bblk), D_STATE:2 * D_STATE]
            nxr = a_re_b * xr - a_im_b * xi + b_re
            nxi = a_re_b * xi + a_im_b * xr + b_im
            bu_s[pl.ds(rt, bblk), 0:D_STATE] = nxr
            bu_s[pl.ds(rt, bblk), D_STATE:2 * D_STATE] = nxi
            return nxr, nxi

        xr, xi = lax.fori_loop(0, sb, step, (sre[...], sim[...]), unroll=True)
        sre[...] = xr
        sim[...] = xi
        y = (_mm(bu_s[:, 0:D_STATE], wc_ref[0]) - _mm(bu_s[:, D_STATE:2 * D_STATE], wc_ref[1])
             + d_ref[...] * ut_s[...])
        y = jax.nn.gelu(y)
        y = y * jax.nn.sigmoid(_mm(y, wglu_ref[...]))
        for t in range(sb):
            for s in range(2):
                yb_s[s, pl.ds(t0 + t, bblk, stride=tcp), :] = y[t * bblk:(t + 1) * bblk, s * LANES:(s + 1) * LANES]

    def group_body(g, carry):
        b0 = pl.multiple_of(g * gsz, gsz)
        r0 = pl.multiple_of(g * rgrp, rgrp)
        hn = hn_s[pl.ds(r0, rgrp), :]
        for c in range(0, COL_B - COL_A, MXU_N):
            pa_s[:, c:c + MXU_N] = jnp.dot(hn, win_ref[:, COL_A + c:COL_A + c + MXU_N],
                                           preferred_element_type=F32)
        for c in range(0, D_IN - COL_C, MXU_N):
            pc_s[:, c:c + MXU_N] = jnp.dot(hn, win_ref[:, COL_C + c:COL_C + c + MXU_N],
                                           preferred_element_type=F32)

        def batch_body(j, carry2):
            b = b0 + j
            rj = pl.multiple_of(j * tcp, tcp)
            rb = pl.multiple_of(b * tcp, tcp)
            k_cur = rope_a(pa_s[pl.ds(rj, tcp), D_ATTN:D_ATTN + D_KV])
            v_cur = pa_s[pl.ds(rj, tcp), D_ATTN + D_KV:D_ATTN + 2 * D_KV]
            if tcp == WINDOW:
                kout_ref[b] = k_cur
                vout_ref[b] = v_cur
            else:
                keep = lax.broadcasted_iota(jnp.int32, (tcp, LANES), 0) < tcp - tc
                for buf, cur, out in ((kbuf, k_cur, kout_ref), (vbuf, v_cur, vout_ref)):
                    shifted = pltpu.roll(buf[b, 0:WINDOW], WINDOW - tc, 0)
                    out[b, 0:WINDOW - tcp] = shifted[0:WINDOW - tcp]
                    out[b, WINDOW - tcp:WINDOW] = jnp.where(keep, shifted[WINDOW - tcp:WINDOW],
                                                            pltpu.roll(cur, tcp - tc, 0))
            kbuf[b, WINDOW:WINDOW + tcp] = k_cur
            vbuf[b, WINDOW:WINDOW + tcp] = v_cur
            def q_pair(pj):
                return rope_a(pa_s[pl.ds(rj, tcp), pj * LANES:(pj + 1) * LANES])

            def softmax_matmul(q_grp, heads, k_op, v_op):
                s_grp = _mm_nt(q_grp * (HEAD_DIM ** -0.5 * LOG2E), k_op)
                ex, sink_w = [], []
                for t, hd in enumerate(heads):
                    s = jnp.where(valid, s_grp[t * tcp:(t + 1) * tcp], -jnp.inf)
                    sk = sink_ref[layer0 + lyr, hd] * LOG2E
                    m = jnp.maximum(jnp.max(s, axis=-1, keepdims=True), sk)
                    ex.append(jnp.exp2(s - m))
                    sink_w.append(jnp.exp2(sk - m))
                r = _mm(jnp.concatenate(ex, axis=0), jnp.concatenate([v_op.astype(BF16), ones_k], axis=1))
                o_grp = r[:, 0:LANES] / (r[:, LANES:2 * LANES] + jnp.concatenate(sink_w, axis=0))
                return [o_grp[t * tcp:(t + 1) * tcp] for t in range(len(heads))]

            k_cat, v_cat = kbuf[b], vbuf[b]
            hi_q = jnp.logical_not(lo_q)
            o_heads = [None] * N_Q_HEADS
            if att_group == N_Q_HEADS:
                q_rows = []
                for pj in range(N_Q_HEADS // 2):
                    qp = q_pair(pj)
                    qx = pltpu.roll(qp, HEAD_DIM, 1)
                    kv_lo = (pj // 2 == 0)
                    q_even, q_odd = (qp, qx) if kv_lo else (qx, qp)
                    keep = lo_q if kv_lo else hi_q
                    q_rows += [jnp.where(keep, q_even, 0.0), jnp.where(keep, q_odd, 0.0)]
                o_heads = softmax_matmul(jnp.concatenate(q_rows, axis=0), range(N_Q_HEADS), k_cat, v_cat)
                for hd in range(N_Q_HEADS):
                    if (hd % 2 == 0) != (hd // Q_PER_KV == 0):
                        o_heads[hd] = pltpu.roll(o_heads[hd], HEAD_DIM, 1)
            else:
                k_rol, v_rol = pltpu.roll(k_cat, HEAD_DIM, 1), pltpu.roll(v_cat, HEAD_DIM, 1)
                for kv in range(N_KV_HEADS):
                    qs = jnp.concatenate([q_pair(2 * kv), q_pair(2 * kv + 1)], axis=0)
                    qs = (qs * (HEAD_DIM ** -0.5 * LOG2E)).astype(BF16)
                    o = jnp.zeros((2 * tcp, LANES), F32)
                    for e in range(2):
                        keep = lo_k if e == 0 else jnp.logical_not(lo_k)
                        k_op = jnp.where(keep, k_cat if e == kv else k_rol, 0.0)
                        v_op = jnp.where(keep, v_cat if e == kv else v_rol, 0.0)
                        s = jnp.where(valid2, _mm_nt(qs, k_op), -jnp.inf)
                        sk = jnp.where(first_head, sink_ref[layer0 + lyr, 4 * kv + e],
                                       sink_ref[layer0 + lyr, 4 * kv + 2 + e]) * LOG2E
                        m = jnp.maximum(jnp.max(s, axis=-1, keepdims=True), sk)
                        r = _mm(jnp.exp2(s - m), jnp.concatenate([v_op.astype(BF16), ones_k], axis=1))
                        o = o + r[:, 0:LANES] / (r[:, LANES:2 * LANES] + jnp.exp2(sk - m))
                    o_heads[4 * kv:4 * kv + 4] = [o[0:tcp], o[0:tcp], o[tcp:2 * tcp], o[tcp:2 * tcp]]
            for pj in range(N_Q_HEADS // 2):
                ga = pa_s[pl.ds(rj, tcp),
                          D_ATTN + 2 * D_KV + pj * LANES:D_ATTN + 2 * D_KV + (pj + 1) * LANES]
                ya_s[pl.ds(rb, tcp), pj * LANES:(pj + 1) * LANES] = (
                    jnp.where(lo_q, o_heads[2 * pj], o_heads[2 * pj + 1]) * _silu(ga))
            if n_chunks > 1:
                kbuf[b, 0:WINDOW] = k_cur
                vbuf[b, 0:WINDOW] = v_cur
            for pr in range(n_rg):
                cs = slice(pr * w_ret, (pr + 1) * w_ret)

                def rope_cols(c0):
                    return jnp.concatenate([rope_r(pc_s[pl.ds(rj, tcp), c0 + t * LANES:c0 + (t + 1) * LANES])
                                            for t in range(w_ret // LANES)], axis=1)

                q = rope_cols(pr * w_ret)
                k = rope_cols(D_RET + pr * w_ret) * (HEAD_DIM ** -0.5)
                v = pc_s[pl.ds(rj, tcp), 2 * D_RET + pr * w_ret:2 * D_RET + (pr + 1) * w_ret]
                gr = pc_s[pl.ds(rj, tcp), 3 * D_RET + pr * w_ret:3 * D_RET + (pr + 1) * w_ret]
                r_old = rst[b, pr]
                k_heads = jnp.concatenate([jnp.where(ret_lane_head == hd, k, 0.0) for hd in range(ret_group)], axis=0)
                v_heads = jnp.concatenate([jnp.where(ret_lane_head == hd, v, 0.0) for hd in range(ret_group)], axis=0)
                sc = _mm_nt(q, k_heads) * dmask_ref[pr]
                o = _mm(q, r_old) * rdec_ref[1, :, cs] + _mm(sc, v_heads)
                kd = k * rdec_ref[0, :, cs]
                if tcp < WINDOW:
                    pad = jnp.zeros((WINDOW - tcp, w_ret), F32)
                    d_new = _mm(jnp.concatenate([kd, pad], axis=0).T, jnp.concatenate([v, pad], axis=0))
                else:
                    d_new = _mm(kd.T, v)
                rst[b, pr] = r_old * rdec_ref[2, 0:1, cs] + jnp.where(same_head, d_new, 0.0)
                o2 = o * o
                o2_hi = o2.astype(BF16)
                o2_lo = (o2 - o2_hi.astype(F32)).astype(BF16)
                msq = jnp.dot(jnp.concatenate([o2_hi, o2_lo], axis=1), headmean2, preferred_element_type=F32)
                yc_s[pl.ds(rb, tcp), cs] = o * lax.rsqrt(msq + EPS) * rg_ref[:, cs] * _silu(gr)
            return carry2

        lax.fori_loop(0, gsz, batch_body, 0, unroll=min(gsz, 8))
        s5_sub_block(g)
        return carry

    lax.fori_loop(0, n_groups, group_body, 0, unroll=min(n_groups, 2))

    def out_body(g, carry):
        b0 = pl.multiple_of(g * gsz, gsz)
        r0 = pl.multiple_of(g * rgrp, rgrp)
        ya = ya_s[pl.ds(r0, rgrp), :].astype(BF16)
        gate_b = _silu(gb_s[pl.ds(r0, rgrp), :])
        yb0 = (yb_s[0, pl.ds(r0, rgrp), :] * gate_b[:, 0:LANES]).astype(BF16)
        yb1 = (yb_s[1, pl.ds(r0, rgrp), :] * gate_b[:, LANES:2 * LANES]).astype(BF16)
        yc = yc_s[pl.ds(r0, rgrp), :].astype(BF16)
        for c in range(0, D_MODEL, MXU_N):
            cc = slice(c, c + MXU_N)
            y = (jnp.dot(ya, wout_ref[0:D_ATTN, cc], preferred_element_type=F32)
                 + jnp.dot(yb0, wout_ref[D_ATTN:D_ATTN + LANES, cc], preferred_element_type=F32)
                 + jnp.dot(yb1, wout_ref[D_ATTN + LANES:D_ATTN + D_SSM, cc], preferred_element_type=F32)
                 + jnp.dot(yc, wout_ref[D_ATTN + D_SSM:D_MODEL, cc], preferred_element_type=F32))
            h1 = h_rows(b0, cc).reshape(rgrp, MXU_N) + y
            h1_s[:, cc] = h1
            h1b_s[:, cc] = h1.astype(BF16)
        pb = p_ref[pl.ds(b0, gsz)].reshape(rgrp, PLE_DIM).astype(BF16)
        ssq = jnp.zeros((rgrp, 1), F32)
        for c in range(0, D_MODEL, MXU_N):
            cc = slice(c, c + MXU_N)
            gate = jax.nn.sigmoid(jnp.dot(h1b_s[...], wgate_ref[:, cc], preferred_element_type=F32))
            pp = jnp.dot(pb, wpp_ref[:, cc], preferred_element_type=F32)
            h2 = h1_s[:, cc] + gate * pp
            if final_norm:
                ssq = ssq + jnp.sum(h2 * h2, axis=-1, keepdims=True)
                h1_s[:, cc] = h2
            else:
                hout_ref[pl.ds(b0, gsz), :, cc] = h2.reshape(gsz, tcp, MXU_N)

        def write_normed():
            scale = lax.rsqrt(ssq * (1.0 / D_MODEL) + EPS)
            hout_ref[pl.ds(b0, gsz)] = (h1_s[...] * scale * fg_ref[...]).reshape(gsz, tcp, D_MODEL)

        if resident:
            h_all[pl.ds(hb + b0, gsz)] = h1_s[...].reshape(gsz, tcp, D_MODEL)
            pl.when(lyr == n_layers - 1)(write_normed)
        elif final_norm:
            write_normed()
        return carry

    lax.fori_loop(0, n_groups, out_body, 0)

    @pl.when(n == n_chunks - 1)
    def _finish():
        sreo_ref[...] = sre[...]
        simo_ref[...] = sim[...]
        r_fin = rst[...]
        ro_ref[...] = sum(r_fin[..., hd * HEAD_DIM:(hd + 1) * HEAD_DIM] for hd in range(ret_group)).reshape(
            bblk, N_RET_HEADS * HEAD_DIM, HEAD_DIM)


def _rope_tables(pos0, t_pad):
    pos = jnp.arange(t_pad, dtype=F32) + pos0
    lane = np.arange(LANES) % HEAD_DIM

    def tables(theta, rot_dim):
        half = rot_dim // 2
        inv = 1.0 / jnp.power(theta, jnp.arange(half, dtype=F32) / half)
        ang = pos[:, None] * inv[None, :]
        cos = jnp.cos(ang)
        sin = jnp.sin(ang)
        idx = np.where(lane < rot_dim, lane % half, 0)
        in_rot = jnp.asarray(lane < rot_dim)
        is_hi = jnp.asarray((lane >= half) & (lane < rot_dim))
        is_lo = jnp.asarray(lane < half)
        cos_l = jnp.where(in_rot[None, :], cos[:, idx], 1.0)
        sin_l = sin[:, idx]
        return (cos_l, jnp.where(is_hi[None, :], sin_l, 0.0), jnp.where(is_lo[None, :], -sin_l, 0.0))

    return jnp.stack(tables(ROPE_THETA, ROT_DIM) + tables(RET_THETA, HEAD_DIM))


def _retention_tables(tc, tcp, ret_group):
    gammas = 1.0 - jnp.power(2.0, -5.0 - jnp.arange(N_RET_HEADS, dtype=F32))
    log_g = jnp.log(gammas)
    i = jnp.arange(tcp, dtype=F32)
    rel = i[:, None] - i[None, :]
    dmask = jnp.where(rel[None] >= 0, jnp.exp(log_g[:, None, None] * jnp.maximum(rel, 0.0)[None]), 0.0)
    kdec = jnp.exp(log_g[None, :] * (tc - 1.0 - i)[:, None])
    qdec = jnp.exp(log_g[None, :] * (i + 1.0)[:, None])
    gch = jnp.broadcast_to(jnp.exp(log_g * tc)[None, :], (tcp, N_RET_HEADS))
    rdec = jnp.stack([jnp.repeat(t, HEAD_DIM, axis=1) for t in (kdec, qdec, gch)])
    dmask = dmask.reshape(N_RET_HEADS // ret_group, ret_group, tcp, tcp).transpose(0, 2, 1, 3).reshape(
        N_RET_HEADS // ret_group, tcp, ret_group * tcp)
    return dmask, rdec


def _layer_call(cfg, batch, h, p_all, rope, dmask, rdec, kwin, vwin, sre0, sim0, r0, weights):
    bblk, tc, tcp, n_chunks, pos0, layer0, n_layers, final_norm, att_group, ret_group = cfg
    w_ret = ret_group * HEAD_DIM
    resident_h = n_layers > 1
    assert not resident_h or (n_chunks == 1 and final_norm)
    rows = bblk * tcp
    rgrp = min(ROW_GROUP, rows)
    rows_sb = min(tc, rgrp // bblk) * bblk
    nb = batch // bblk
    t_pad = n_chunks * tcp
    const = pl.Buffered(1)

    def wspec(shape):
        nd = len(shape)
        return pl.BlockSpec((None,) + tuple(shape[1:]), lambda l, bb, n: (layer0 + l,) + (0,) * (nd - 1),
                            pipeline_mode=const)

    def sspec(shape, lead_layer, mode=None):
        nd = len(shape)
        if lead_layer:
            return pl.BlockSpec((None, bblk) + tuple(shape[2:]),
                                lambda l, bb, n: (layer0 + l, bb) + (0,) * (nd - 2), pipeline_mode=mode)
        return pl.BlockSpec((bblk,) + tuple(shape[1:]), lambda l, bb, n: (bb,) + (0,) * (nd - 1),
                            pipeline_mode=mode)

    big_in_mode = const if resident_h else None

    def ospec(tail):
        return pl.BlockSpec((None, bblk) + tail, lambda l, bb, n: (l, bb) + (0,) * len(tail))

    if resident_h:
        h_in_map = lambda l, bb, n: (jnp.where(l == 0, bb, 0), 0, 0)
        h_out_map = lambda l, bb, n: (jnp.where(l == n_layers - 1, bb, 0), 0, 0)
    else:
        h_in_map = h_out_map = lambda l, bb, n: (bb, n, 0)

    (sink, ng, w_in, lam, bbd, wc, d_row, wglu, rg, w_out, w_pp, w_gate, fg) = weights
    in_specs = [
        pl.BlockSpec((bblk, tcp, D_MODEL), h_in_map, pipeline_mode=big_in_mode),
        pl.BlockSpec((None, bblk, tcp, PLE_DIM), lambda l, bb, n: (layer0 + l, bb, n, 0)),
        pl.BlockSpec((6, tcp, LANES), lambda l, bb, n: (0, n, 0)),
        pl.BlockSpec(dmask.shape, lambda l, bb, n: (0, 0, 0), pipeline_mode=const),
        pl.BlockSpec(rdec.shape, lambda l, bb, n: (0, 0, 0), pipeline_mode=const),
        sspec(kwin.shape, kwin.ndim == 4), sspec(vwin.shape, vwin.ndim == 4),
        sspec(sre0.shape, sre0.ndim == 3), sspec(sim0.shape, sim0.ndim == 3),
        sspec(r0.shape, r0.ndim == 4),
        pl.BlockSpec(memory_space=pltpu.SMEM),
        wspec(ng.shape), wspec(w_in.shape), wspec(lam.shape), wspec(bbd.shape), wspec(wc.shape),
        wspec(d_row.shape), wspec(wglu.shape), wspec(rg.shape), wspec(w_out.shape), wspec(w_pp.shape),
        wspec(w_gate.shape),
        pl.BlockSpec(fg.shape, lambda l, bb, n: (0, 0), pipeline_mode=const),
    ]
    out_shape = (
        jax.ShapeDtypeStruct((batch, t_pad, D_MODEL), F32),
        jax.ShapeDtypeStruct((n_layers, batch, WINDOW, LANES), F32),
        jax.ShapeDtypeStruct((n_layers, batch, WINDOW, LANES), F32),
        jax.ShapeDtypeStruct((n_layers, batch, D_STATE), F32),
        jax.ShapeDtypeStruct((n_layers, batch, D_STATE), F32),
        jax.ShapeDtypeStruct((n_layers, batch, N_RET_HEADS * HEAD_DIM, HEAD_DIM), F32),
    )
    out_specs = (
        pl.BlockSpec((bblk, tcp, D_MODEL), h_out_map),
        ospec((WINDOW, LANES)), ospec((WINDOW, LANES)), ospec((D_STATE,)), ospec((D_STATE,)),
        ospec((N_RET_HEADS * HEAD_DIM, HEAD_DIM)),
    )
    scratch = [
        pltpu.VMEM((bblk, 2 * WINDOW, LANES), F32),
        pltpu.VMEM((bblk, 2 * WINDOW, LANES), F32),
        pltpu.VMEM((bblk, D_STATE), F32),
        pltpu.VMEM((bblk, D_STATE), F32),
        pltpu.VMEM((bblk, N_RET_HEADS // ret_group, w_ret, w_ret), F32),
        pltpu.VMEM((D_SSM, 2 * D_STATE), BF16),
        pltpu.VMEM((2, D_STATE), F32),
        pltpu.VMEM((rows, D_MODEL), BF16),
        pltpu.VMEM((rgrp, COL_B - COL_A), F32),
        pltpu.VMEM((rgrp, D_IN - COL_C), F32),
        pltpu.VMEM((2, rows, LANES), F32),
        pltpu.VMEM((rows, D_SSM), F32),
        pltpu.VMEM((rows, D_ATTN), F32),
        pltpu.VMEM((rows, D_RET), F32),
        pltpu.VMEM((2, rows, LANES), F32),
        pltpu.VMEM((rows_sb, D_SSM), F32),
        pltpu.VMEM((rows_sb, 2 * D_STATE), F32),
        pltpu.VMEM((rgrp, D_MODEL), F32),
        pltpu.VMEM((rgrp, D_MODEL), BF16),
    ]
    if resident_h:
        scratch.append(pltpu.VMEM((batch, tcp, D_MODEL), F32))
    return pl.pallas_call(
        functools.partial(_layer_kernel, cfg),
        grid=(n_layers, nb, n_chunks),
        in_specs=in_specs,
        out_specs=out_specs,
        out_shape=out_shape,
        scratch_shapes=scratch,
        compiler_params=pltpu.CompilerParams(
            dimension_semantics=("arbitrary", "arbitrary", "arbitrary"),
            vmem_limit_bytes=VMEM_LIMIT),
        name="sample_layers" if resident_h else "prompt_layer_%d" % layer0,
    )(h, p_all, rope, dmask, rdec, kwin, vwin, sre0, sim0, r0,
      sink, ng, w_in, lam, bbd, wc, d_row, wglu, rg, w_out, w_pp, w_gate, fg)


def kernel(x_prompt, x_sample, p_prompt, p_sample, cache_win_k, cache_win_v, state_ssm_re, state_ssm_im,
           state_ret, norm_g, w_in, attn_sink, ssm_lam_re, ssm_lam_im, ssm_log_dt, ssm_b_re, ssm_b_im,
           ssm_c_re, ssm_c_im, ssm_d, w_glu, ret_norm_g, w_out, w_ple_proj, w_ple_gate, final_norm_g):
    depth = w_in.shape[0]
    bp, seq = x_prompt.shape[0], x_prompt.shape[1]
    bs, dec_seq = x_sample.shape[0], x_sample.shape[1]
    past_len = 8192

    same_group = jnp.asarray(np.arange(D_SSM)[:, None] // SSM_GROUP == np.arange(D_STATE)[None, :] // SSM_STATE)
    bbd = jnp.stack([
        jnp.where(same_group, jnp.tile(b.transpose(0, 1, 3, 2).reshape(depth, D_SSM, SSM_STATE),
                                       (1, 1, N_SSM_GROUPS)), 0.0)
        for b in (ssm_b_re, ssm_b_im)], axis=1)
    wc = jnp.stack([
        jnp.where(same_group.T, jnp.tile(c.transpose(0, 1, 3, 2).reshape(depth, D_STATE, SSM_GROUP),
                                         (1, 1, N_SSM_GROUPS)), 0.0).astype(BF16)
        for c in (ssm_c_re, ssm_c_im)], axis=1)
    lam = jnp.stack([ssm_lam_re.reshape(depth, D_STATE), ssm_lam_im.reshape(depth, D_STATE),
                     jnp.repeat(ssm_log_dt, SSM_STATE, axis=-1)], axis=1)
    weights = (
        attn_sink,
        norm_g.reshape(depth, 1, D_MODEL), w_in.astype(BF16), lam, bbd, wc,
        ssm_d.reshape(depth, 1, D_SSM), w_glu.astype(BF16), ret_norm_g.reshape(depth, 1, D_RET),
        w_out.astype(BF16), w_ple_proj.astype(BF16), w_ple_gate.astype(BF16),
        final_norm_g.reshape(1, D_MODEL),
    )

    tc_p = WINDOW
    rope_p = _rope_tables(0.0, seq)
    dmask_p, rdec_p = _retention_tables(tc_p, tc_p, GROUPS_PROMPT[1])
    zk = jnp.zeros((bp, WINDOW, LANES), F32)
    zs = jnp.zeros((bp, D_STATE), F32)
    zr = jnp.zeros((bp, N_RET_HEADS * HEAD_DIM, HEAD_DIM), F32)

    tcp_s = 8
    bblk_s = 16
    rope_s = _rope_tables(float(past_len), tcp_s)
    dmask_s, rdec_s = _retention_tables(dec_seq, tcp_s, GROUPS_SAMPLE[1])
    pad_t = ((0, 0), (0, tcp_s - dec_seq), (0, 0))
    hs = jnp.pad(x_sample, pad_t)
    ps = jnp.pad(p_sample, ((0, 0),) + pad_t)
    kwin_s = cache_win_k.reshape(depth, bs, WINDOW, LANES)
    vwin_s = cache_win_v.reshape(depth, bs, WINDOW, LANES)
    sre_s = state_ssm_re.reshape(depth, bs, D_STATE)
    sim_s = state_ssm_im.reshape(depth, bs, D_STATE)
    r_s = state_ret.reshape(depth, bs, N_RET_HEADS * HEAD_DIM, HEAD_DIM)

    hp = x_prompt
    outs_p = []
    for i in range(depth):
        cfg_p = (bp, tc_p, tc_p, seq // tc_p, 0, i, 1, i == depth - 1) + GROUPS_PROMPT
        hp, *st = _layer_call(cfg_p, bp, hp, p_prompt, rope_p, dmask_p, rdec_p, zk, zk, zs, zs, zr, weights)
        outs_p.append(st)
    st_p = [jnp.concatenate(t, axis=0) for t in zip(*outs_p)]
    cfg_s = (bblk_s, dec_seq, tcp_s, 1, past_len, 0, depth, True) + GROUPS_SAMPLE
    hs, *st_s = _layer_call(cfg_s, bs, hs, ps, rope_s, dmask_s, rdec_s, kwin_s, vwin_s, sre_s, sim_s, r_s, weights)

    def pack(st, b):
        k, v, sr, si, r = st
        return (k.reshape(depth, b, WINDOW, N_KV_HEADS, HEAD_DIM), v.reshape(depth, b, WINDOW, N_KV_HEADS, HEAD_DIM),
                sr.reshape(depth, b, N_SSM_GROUPS, SSM_STATE), si.reshape(depth, b, N_SSM_GROUPS, SSM_STATE),
                r.reshape(depth, b, N_RET_HEADS, HEAD_DIM, HEAD_DIM))

    return (hp, hs[:, :dec_seq]) + pack(st_p, bp) + pack(st_s, bs)
```

```python
import functools
import math

import numpy as np
import jax
import jax.numpy as jnp
from jax import lax
from jax.experimental import pallas as pl
from jax.experimental.pallas import tpu as pltpu

F32 = jnp.float32
BF16 = jnp.bfloat16

D_MODEL = 1024
HEAD_DIM = 64
D_ATTN = 512
N_Q_HEADS = 8
N_KV_HEADS = 2
Q_PER_KV = N_Q_HEADS // N_KV_HEADS
D_KV = 128
WINDOW = 128
ROPE_THETA = 500000.0
ROT_DIM = 16
D_SSM = 256
SSM_GROUP = 16
N_SSM_GROUPS = 16
SSM_STATE = 64
D_STATE = N_SSM_GROUPS * SSM_STATE
D_RET = 256
N_RET_HEADS = 4
RET_THETA = 10000.0
PLE_DIM = 256
EPS = 1e-6
LOG2E = math.log2(math.e)
D_IN = 2816
COL_A = 0
COL_B = 1280
COL_C = 1792
LANES = 128
MXU_N = 256
ROW_GROUP = 256
VMEM_LIMIT = 58 * 1024 * 1024
GROUPS_PROMPT = (2, 2)
GROUPS_SAMPLE = (8, 2)


def _mm(a, b):
    return jnp.dot(a.astype(BF16), b.astype(BF16), preferred_element_type=F32)


def _mm_nt(a, b):
    return lax.dot_general(a.astype(BF16), b.astype(BF16), (((1,), (1,)), ((), ())),
                           preferred_element_type=F32)


def _silu(x):
    return x * jax.nn.sigmoid(x)


def _layer_kernel(cfg,
                  h_ref, p_ref, rope_ref, dmask_ref, rdec_ref,
                  kwin_ref, vwin_ref, sre0_ref, sim0_ref, r0_ref,
                  sink_ref, ng_ref, win_ref, lam_ref, bbd_ref, wc_ref, d_ref, wglu_ref, rg_ref,
                  wout_ref, wpp_ref, wgate_ref, fg_ref,
                  hout_ref, kout_ref, vout_ref, sreo_ref, simo_ref, ro_ref,
                  kbuf, vbuf, sre, sim, rst, wb_s, a_s, hn_s, pa_s, pc_s, u_s, gb_s,
                  ya_s, yc_s, yb_s, ut_s, bu_s, h1_s, h1b_s, *resident):
    bblk, tc, tcp, n_chunks, pos0, layer0, n_layers, final_norm, att_group, ret_group = cfg
    w_ret = ret_group * HEAD_DIM
    n_rg = N_RET_HEADS // ret_group
    rgrp = min(ROW_GROUP, bblk * tcp)
    gsz = rgrp // tcp
    n_groups = bblk // gsz
    sb = min(tc, rgrp // bblk)
    assert tc // sb == n_groups
    assert tcp & (tcp - 1) == 0
    lyr = pl.program_id(0)
    n = pl.program_id(2)
    if resident:
        h_all, = resident
        hb = pl.multiple_of(pl.program_id(1) * bblk, bblk)

        @pl.when(lyr == 0)
        def _load_h():
            h_all[pl.ds(hb, bblk)] = h_ref[...]

        def h_rows(b0, cols=slice(None)):
            return h_all[pl.ds(hb + b0, gsz), :, cols]
    else:
        def h_rows(b0, cols=slice(None)):
            return h_ref[pl.ds(b0, gsz), :, cols]

    def head_of(shape, dim):
        return lax.broadcasted_iota(jnp.int32, shape, dim) >> (HEAD_DIM.bit_length() - 1)

    same_head = head_of((w_ret, w_ret), 0) == head_of((w_ret, w_ret), 1)

    @pl.when(n == 0)
    def _init():
        kbuf[:, 0:WINDOW] = kwin_ref[...]
        vbuf[:, 0:WINDOW] = vwin_ref[...]
        if tcp < WINDOW:
            kbuf[:, WINDOW:2 * WINDOW] = jnp.zeros((bblk, WINDOW, LANES), F32)
            vbuf[:, WINDOW:2 * WINDOW] = jnp.zeros((bblk, WINDOW, LANES), F32)
        sre[...] = sre0_ref[...]
        sim[...] = sim0_ref[...]
        r_in = r0_ref[...].reshape(bblk, n_rg, w_ret, HEAD_DIM)
        rst[...] = jnp.where(same_head, jnp.concatenate([r_in] * ret_group, axis=-1), 0.0)
        if tc < tcp:
            yb_s[...] = jnp.zeros_like(yb_s)

    @pl.when((n == 0) & (pl.program_id(1) == 0))
    def _discretise():
        lr = lam_ref[0:1, :]
        li = lam_ref[1:2, :]
        dt = jnp.exp(lam_ref[2:3, :])
        mag = jnp.exp(lr * dt)
        a_re = mag * jnp.cos(li * dt)
        a_im = mag * jnp.sin(li * dt)
        nr = a_re - 1.0
        ni = a_im
        den = lr * lr + li * li
        coef_re = (nr * lr + ni * li) / den
        coef_im = (ni * lr - nr * li) / den
        a_s[0:1, :] = a_re
        a_s[1:2, :] = a_im
        br = bbd_ref[0]
        bi = bbd_ref[1]
        wb_s[:, 0:D_STATE] = (coef_re * br - coef_im * bi).astype(BF16)
        wb_s[:, D_STATE:2 * D_STATE] = (coef_re * bi + coef_im * br).astype(BF16)

    lo_q = head_of((tcp, LANES), 1) == 0
    headmean = jnp.where(same_head, 1.0 / HEAD_DIM, 0.0).astype(BF16)
    headmean2 = jnp.concatenate([headmean, headmean], axis=0)
    ones_k = jnp.ones((2 * WINDOW, LANES), BF16)
    ret_lane_head = head_of((tcp, w_ret), 1)

    cos_a, sa_hi, sa_lo = rope_ref[0], rope_ref[1], rope_ref[2]
    cos_r, sr_hi, sr_lo = rope_ref[3], rope_ref[4], rope_ref[5]

    def rope_a(x):
        return (x * cos_a + pltpu.roll(x, ROT_DIM // 2, 1) * sa_hi
                + pltpu.roll(x, LANES - ROT_DIM // 2, 1) * sa_lo)

    def rope_r(x):
        return (x * cos_r + pltpu.roll(x, HEAD_DIM // 2, 1) * sr_hi
                + pltpu.roll(x, LANES - HEAD_DIM // 2, 1) * sr_lo)

    lmin = WINDOW - pos0 - n * tc

    def valid_keys(n_heads):
        qi = lax.broadcasted_iota(jnp.int32, (n_heads * tcp, 2 * WINDOW), 0) & (tcp - 1)
        kl = lax.broadcasted_iota(jnp.int32, (n_heads * tcp, 2 * WINDOW), 1)
        return (kl > qi) & (kl <= qi + WINDOW) & (kl >= lmin)

    valid = valid_keys(1)
    valid2 = valid_keys(2)
    first_head = lax.broadcasted_iota(jnp.int32, (2 * tcp, 1), 0) < tcp
    lo_k = head_of((2 * WINDOW, LANES), 1) == 0

    def norm_body(g, carry):
        b0 = pl.multiple_of(g * gsz, gsz)
        r0 = pl.multiple_of(g * rgrp, rgrp)
        x = h_rows(b0).reshape(rgrp, D_MODEL)
        ms = jnp.mean(x * x, axis=-1, keepdims=True)
        hn = (x * lax.rsqrt(ms + EPS) * ng_ref[...]).astype(BF16)
        hn_s[pl.ds(r0, rgrp), :] = hn
        ub = jnp.dot(hn, win_ref[:, COL_B:COL_B + D_SSM], preferred_element_type=F32)
        u_s[0, pl.ds(r0, rgrp), :] = ub[:, 0:LANES]
        u_s[1, pl.ds(r0, rgrp), :] = ub[:, LANES:2 * LANES]
        gb_s[pl.ds(r0, rgrp), :] = jnp.dot(hn, win_ref[:, COL_B + D_SSM:COL_C],
                                                preferred_element_type=F32)
        return carry

    lax.fori_loop(0, n_groups, norm_body, 0, unroll=True)

    a_re_b = jnp.broadcast_to(a_s[0:1, :], (bblk, D_STATE))
    a_im_b = jnp.broadcast_to(a_s[1:2, :], (bblk, D_STATE))

    def s5_sub_block(si):
        t0 = si * sb
        for t in range(sb):
            for s in range(2):
                ut_s[t * bblk:(t + 1) * bblk, s * LANES:(s + 1) * LANES] = (
                    u_s[s, pl.ds(t0 + t, bblk, stride=tcp), :])
        for c in range(0, 2 * D_STATE, MXU_N):
            bu_s[:, c:c + MXU_N] = _mm(ut_s[...], wb_s[:, c:c + MXU_N])

        def step(t, xs):
            xr, xi = xs
            rt = pl.multiple_of(t * bblk, bblk)
            b_re = bu_s[pl.ds(rt, bblk), 0:D_STATE]
            b_im = bu_s[pl.ds(rt, bblk), D_STATE:2 * D_STATE]
            nxr = a_re_b * xr - a_im_b * xi + b_re
            nxi = a_re_b * xi + a_im_b * xr + b_im
            bu_s[pl.ds(rt, bblk), 0:D_STATE] = nxr
            bu_s[pl.ds(rt, bblk), D_STATE:2 * D_STATE] = nxi
            return nxr, nxi

        xr, xi = lax.fori_loop(0, sb, step, (sre[...], sim[...]), unroll=True)
        sre[...] = xr
        sim[...] = xi
        y = (_mm(bu_s[:, 0:D_STATE], wc_ref[0]) - _mm(bu_s[:, D_STATE:2 * D_STATE], wc_ref[1])
             + d_ref[...] * ut_s[...])
        y = jax.nn.gelu(y)
        y = y * jax.nn.sigmoid(_mm(y, wglu_ref[...]))
        for t in range(sb):
            for s in range(2):
                yb_s[s, pl.ds(t0 + t, bblk, stride=tcp), :] = y[t * bblk:(t + 1) * bblk, s * LANES:(s + 1) * LANES]

    def group_body(g, carry):
        b0 = pl.multiple_of(g * gsz, gsz)
        r0 = pl.multiple_of(g * rgrp, rgrp)
        hn = hn_s[pl.ds(r0, rgrp), :]
        for c in range(0, COL_B - COL_A, MXU_N):
            pa_s[:, c:c + MXU_N] = jnp.dot(hn, win_ref[:, COL_A + c:COL_A + c + MXU_N],
                                           preferred_element_type=F32)
        for c in range(0, D_IN - COL_C, MXU_N):
            pc_s[:, c:c + MXU_N] = jnp.dot(hn, win_ref[:, COL_C + c:COL_C + c + MXU_N],
                                           preferred_element_type=F32)

        def batch_body(j, carry2):
            b = b0 + j
            rj = pl.multiple_of(j * tcp, tcp)
            rb = pl.multiple_of(b * tcp, tcp)
            k_cur = rope_a(pa_s[pl.ds(rj, tcp), D_ATTN:D_ATTN + D_KV])
            v_cur = pa_s[pl.ds(rj, tcp), D_ATTN + D_KV:D_ATTN + 2 * D_KV]
            if tcp == WINDOW:
                kout_ref[b] = k_cur
                vout_ref[b] = v_cur
            else:
                keep = lax.broadcasted_iota(jnp.int32, (tcp, LANES), 0) < tcp - tc
                for buf, cur, out in ((kbuf, k_cur, kout_ref), (vbuf, v_cur, vout_ref)):
                    shifted = pltpu.roll(buf[b, 0:WINDOW], WINDOW - tc, 0)
                    out[b, 0:WINDOW - tcp] = shifted[0:WINDOW - tcp]
                    out[b, WINDOW - tcp:WINDOW] = jnp.where(keep, shifted[WINDOW - tcp:WINDOW],
                                                            pltpu.roll(cur, tcp - tc, 0))
            kbuf[b, WINDOW:WINDOW + tcp] = k_cur
            vbuf[b, WINDOW:WINDOW + tcp] = v_cur
            def q_pair(pj):
                return rope_a(pa_s[pl.ds(rj, tcp), pj * LANES:(pj + 1) * LANES])

            def softmax_matmul(q_grp, heads, k_op, v_op):
                s_grp = _mm_nt(q_grp * (HEAD_DIM ** -0.5 * LOG2E), k_op)
                ex, sink_w = [], []
                for t, hd in enumerate(heads):
                    s = jnp.where(valid, s_grp[t * tcp:(t + 1) * tcp], -jnp.inf)
                    sk = sink_ref[layer0 + lyr, hd] * LOG2E
                    m = jnp.maximum(jnp.max(s, axis=-1, keepdims=True), sk)
                    ex.append(jnp.exp2(s - m))
                    sink_w.append(jnp.exp2(sk - m))
                r = _mm(jnp.concatenate(ex, axis=0), jnp.concatenate([v_op.astype(BF16), ones_k], axis=1))
                o_grp = r[:, 0:LANES] / (r[:, LANES:2 * LANES] + jnp.concatenate(sink_w, axis=0))
                return [o_grp[t * tcp:(t + 1) * tcp] for t in range(len(heads))]

            k_cat, v_cat = kbuf[b], vbuf[b]
            hi_q = jnp.logical_not(lo_q)
            o_heads = [None] * N_Q_HEADS
            if att_group == N_Q_HEADS:
                q_rows = []
                for pj in range(N_Q_HEADS // 2):
                    qp = q_pair(pj)
                    qx = pltpu.roll(qp, HEAD_DIM, 1)
                    kv_lo = (pj // 2 == 0)
                    q_even, q_odd = (qp, qx) if kv_lo else (qx, qp)
                    keep = lo_q if kv_lo else hi_q
                    q_rows += [jnp.where(keep, q_even, 0.0), jnp.where(keep, q_odd, 0.0)]
                o_heads = softmax_matmul(jnp.concatenate(q_rows, axis=0), range(N_Q_HEADS), k_cat, v_cat)
                for hd in range(N_Q_HEADS):
                    if (hd % 2 == 0) != (hd // Q_PER_KV == 0):
                        o_heads[hd] = pltpu.roll(o_heads[hd], HEAD_DIM, 1)
            else:
                k_rol, v_rol = pltpu.roll(k_cat, HEAD_DIM, 1), pltpu.roll(v_cat, HEAD_DIM, 1)
                for kv in range(N_KV_HEADS):
                    qs = jnp.concatenate([q_pair(2 * kv), q_pair(2 * kv + 1)], axis=0)
                    qs = (qs * (HEAD_DIM ** -0.5 * LOG2E)).astype(BF16)
                    o = jnp.zeros((2 * tcp, LANES), F32)
                    for e in range(2):
                        keep = lo_k if e == 0 else jnp.logical_not(lo_k)
                        k_op = jnp.where(keep, k_cat if e == kv else k_rol, 0.0)
                        v_op = jnp.where(keep, v_cat if e == kv else v_rol, 0.0)
                        s = jnp.where(valid2, _mm_nt(qs, k_op), -jnp.inf)
                        sk = jnp.where(first_head, sink_ref[layer0 + lyr, 4 * kv + e],
                                       sink_ref[layer0 + lyr, 4 * kv + 2 + e]) * LOG2E
                        m = jnp.maximum(jnp.max(s, axis=-1, keepdims=True), sk)
                        r = _mm(jnp.exp2(s - m), jnp.concatenate([v_op.astype(BF16), ones_k], axis=1))
                        o = o + r[:, 0:LANES] / (r[:, LANES:2 * LANES] + jnp.exp2(sk - m))
                    o_heads[4 * kv:4 * kv + 4] = [o[0:tcp], o[0:tcp], o[tcp:2 * tcp], o[tcp:2 * tcp]]
            for pj in range(N_Q_HEADS // 2):
                ga = pa_s[pl.ds(rj, tcp),
                          D_ATTN + 2 * D_KV + pj * LANES:D_ATTN + 2 * D_KV + (pj + 1) * LANES]
                ya_s[pl.ds(rb, tcp), pj * LANES:(pj + 1) * LANES] = (
                    jnp.where(lo_q, o_heads[2 * pj], o_heads[2 * pj + 1]) * _silu(ga))
            if n_chunks > 1:
                kbuf[b, 0:WINDOW] = k_cur
                vbuf[b, 0:WINDOW] = v_cur
            for pr in range(n_rg):
                cs = slice(pr * w_ret, (pr + 1) * w_ret)

                def rope_cols(c0):
                    return jnp.concatenate([rope_r(pc_s[pl.ds(rj, tcp), c0 + t * LANES:c0 + (t + 1) * LANES])
                                            for t in range(w_ret // LANES)], axis=1)

                q = rope_cols(pr * w_ret)
                k = rope_cols(D_RET + pr * w_ret) * (HEAD_DIM ** -0.5)
                v = pc_s[pl.ds(rj, tcp), 2 * D_RET + pr * w_ret:2 * D_RET + (pr + 1) * w_ret]
                gr = pc_s[pl.ds(rj, tcp), 3 * D_RET + pr * w_ret:3 * D_RET + (pr + 1) * w_ret]
                r_old = rst[b, pr]
                k_heads = jnp.concatenate([jnp.where(ret_lane_head == hd, k, 0.0) for hd in range(ret_group)], axis=0)
                v_heads = jnp.concatenate([jnp.where(ret_lane_head == hd, v, 0.0) for hd in range(ret_group)], axis=0)
                sc = _mm_nt(q, k_heads) * dmask_ref[pr]
                o = _mm(q, r_old) * rdec_ref[1, :, cs] + _mm(sc, v_heads)
                kd = k * rdec_ref[0, :, cs]
                if tcp < WINDOW:
                    pad = jnp.zeros((WINDOW - tcp, w_ret), F32)
                    d_new = _mm(jnp.concatenate([kd, pad], axis=0).T, jnp.concatenate([v, pad], axis=0))
                else:
                    d_new = _mm(kd.T, v)
                rst[b, pr] = r_old * rdec_ref[2, 0:1, cs] + jnp.where(same_head, d_new, 0.0)
                o2 = o * o
                o2_hi = o2.astype(BF16)
                o2_lo = (o2 - o2_hi.astype(F32)).astype(BF16)
                msq = jnp.dot(jnp.concatenate([o2_hi, o2_lo], axis=1), headmean2, preferred_element_type=F32)
                yc_s[pl.ds(rb, tcp), cs] = o * lax.rsqrt(msq + EPS) * rg_ref[:, cs] * _silu(gr)
            return carry2

        lax.fori_loop(0, gsz, batch_body, 0, unroll=min(gsz, 8))
        s5_sub_block(g)
        return carry

    lax.fori_loop(0, n_groups, group_body, 0, unroll=min(n_groups, 4))

    def out_body(g, carry):
        b0 = pl.multiple_of(g * gsz, gsz)
        r0 = pl.multiple_of(g * rgrp, rgrp)
        ya = ya_s[pl.ds(r0, rgrp), :].astype(BF16)
        gate_b = _silu(gb_s[pl.ds(r0, rgrp), :])
        yb0 = (yb_s[0, pl.ds(r0, rgrp), :] * gate_b[:, 0:LANES]).astype(BF16)
        yb1 = (yb_s[1, pl.ds(r0, rgrp), :] * gate_b[:, LANES:2 * LANES]).astype(BF16)
        yc = yc_s[pl.ds(r0, rgrp), :].astype(BF16)
        for c in range(0, D_MODEL, MXU_N):
            cc = slice(c, c + MXU_N)
            y = (jnp.dot(ya, wout_ref[0:D_ATTN, cc], preferred_element_type=F32)
                 + jnp.dot(yb0, wout_ref[D_ATTN:D_ATTN + LANES, cc], preferred_element_type=F32)
                 + jnp.dot(yb1, wout_ref[D_ATTN + LANES:D_ATTN + D_SSM, cc], preferred_element_type=F32)
                 + jnp.dot(yc, wout_ref[D_ATTN + D_SSM:D_MODEL, cc], preferred_element_type=F32))
            h1 = h_rows(b0, cc).reshape(rgrp, MXU_N) + y
            h1_s[:, cc] = h1
            h1b_s[:, cc] = h1.astype(BF16)
        pb = p_ref[pl.ds(b0, gsz)].reshape(rgrp, PLE_DIM).astype(BF16)
        ssq = jnp.zeros((rgrp, 1), F32)
        for c in range(0, D_MODEL, MXU_N):
            cc = slice(c, c + MXU_N)
            gate = jax.nn.sigmoid(jnp.dot(h1b_s[...], wgate_ref[:, cc], preferred_element_type=F32))
            pp = jnp.dot(pb, wpp_ref[:, cc], preferred_element_type=F32)
            h2 = h1_s[:, cc] + gate * pp
            if final_norm:
                ssq = ssq + jnp.sum(h2 * h2, axis=-1, keepdims=True)
                h1_s[:, cc] = h2
            else:
                hout_ref[pl.ds(b0, gsz), :, cc] = h2.reshape(gsz, tcp, MXU_N)

        def write_normed():
            scale = lax.rsqrt(ssq * (1.0 / D_MODEL) + EPS)
            hout_ref[pl.ds(b0, gsz)] = (h1_s[...] * scale * fg_ref[...]).reshape(gsz, tcp, D_MODEL)

        if resident:
            h_all[pl.ds(hb + b0, gsz)] = h1_s[...].reshape(gsz, tcp, D_MODEL)
            pl.when(lyr == n_layers - 1)(write_normed)
        elif final_norm:
            write_normed()
        return carry

    lax.fori_loop(0, n_groups, out_body, 0, unroll=min(n_groups, 2))

    @pl.when(n == n_chunks - 1)
    def _finish():
        sreo_ref[...] = sre[...]
        simo_ref[...] = sim[...]
        r_fin = rst[...]
        ro_ref[...] = sum(r_fin[..., hd * HEAD_DIM:(hd + 1) * HEAD_DIM] for hd in range(ret_group)).reshape(
            bblk, N_RET_HEADS * HEAD_DIM, HEAD_DIM)


def _rope_tables(pos0, t_pad):
    pos = jnp.arange(t_pad, dtype=F32) + pos0
    lane = np.arange(LANES) % HEAD_DIM

    def tables(theta, rot_dim):
        half = rot_dim // 2
        inv = 1.0 / jnp.power(theta, jnp.arange(half, dtype=F32) / half)
        ang = pos[:, None] * inv[None, :]
        cos = jnp.cos(ang)
        sin = jnp.sin(ang)
        idx = np.where(lane < rot_dim, lane % half, 0)
        in_rot = jnp.asarray(lane < rot_dim)
        is_hi = jnp.asarray((lane >= half) & (lane < rot_dim))
        is_lo = jnp.asarray(lane < half)
        cos_l = jnp.where(in_rot[None, :], cos[:, idx], 1.0)
        sin_l = sin[:, idx]
        return (cos_l, jnp.where(is_hi[None, :], sin_l, 0.0), jnp.where(is_lo[None, :], -sin_l, 0.0))

    return jnp.stack(tables(ROPE_THETA, ROT_DIM) + tables(RET_THETA, HEAD_DIM))


def _retention_tables(tc, tcp, ret_group):
    gammas = 1.0 - jnp.power(2.0, -5.0 - jnp.arange(N_RET_HEADS, dtype=F32))
    log_g = jnp.log(gammas)
    i = jnp.arange(tcp, dtype=F32)
    rel = i[:, None] - i[None, :]
    dmask = jnp.where(rel[None] >= 0, jnp.exp(log_g[:, None, None] * jnp.maximum(rel, 0.0)[None]), 0.0)
    kdec = jnp.exp(log_g[None, :] * (tc - 1.0 - i)[:, None])
    qdec = jnp.exp(log_g[None, :] * (i + 1.0)[:, None])
    gch = jnp.broadcast_to(jnp.exp(log_g * tc)[None, :], (tcp, N_RET_HEADS))
    rdec = jnp.stack([jnp.repeat(t, HEAD_DIM, axis=1) for t in (kdec, qdec, gch)])
    dmask = dmask.reshape(N_RET_HEADS // ret_group, ret_group, tcp, tcp).transpose(0, 2, 1, 3).reshape(
        N_RET_HEADS // ret_group, tcp, ret_group * tcp)
    return dmask, rdec


def _layer_call(cfg, batch, h, p_all, rope, dmask, rdec, kwin, vwin, sre0, sim0, r0, weights):
    bblk, tc, tcp, n_chunks, pos0, layer0, n_layers, final_norm, att_group, ret_group = cfg
    w_ret = ret_group * HEAD_DIM
    resident_h = n_layers > 1
    assert not resident_h or (n_chunks == 1 and final_norm)
    rows = bblk * tcp
    rgrp = min(ROW_GROUP, rows)
    rows_sb = min(tc, rgrp // bblk) * bblk
    nb = batch // bblk
    t_pad = n_chunks * tcp
    const = pl.Buffered(1)

    def wspec(shape):
        nd = len(shape)
        return pl.BlockSpec((None,) + tuple(shape[1:]), lambda l, bb, n: (layer0 + l,) + (0,) * (nd - 1),
                            pipeline_mode=const)

    def sspec(shape, lead_layer, mode=None):
        nd = len(shape)
        if lead_layer:
            return pl.BlockSpec((None, bblk) + tuple(shape[2:]),
                                lambda l, bb, n: (layer0 + l, bb) + (0,) * (nd - 2), pipeline_mode=mode)
        return pl.BlockSpec((bblk,) + tuple(shape[1:]), lambda l, bb, n: (bb,) + (0,) * (nd - 1),
                            pipeline_mode=mode)

    big_in_mode = const if resident_h else None

    def ospec(tail):
        return pl.BlockSpec((None, bblk) + tail, lambda l, bb, n: (l, bb) + (0,) * len(tail))

    if resident_h:
        h_in_map = lambda l, bb, n: (jnp.where(l == 0, bb, 0), 0, 0)
        h_out_map = lambda l, bb, n: (jnp.where(l == n_layers - 1, bb, 0), 0, 0)
    else:
        h_in_map = h_out_map = lambda l, bb, n: (bb, n, 0)

    (sink, ng, w_in, lam, bbd, wc, d_row, wglu, rg, w_out, w_pp, w_gate, fg) = weights
    in_specs = [
        pl.BlockSpec((bblk, tcp, D_MODEL), h_in_map, pipeline_mode=big_in_mode),
        pl.BlockSpec((None, bblk, tcp, PLE_DIM), lambda l, bb, n: (layer0 + l, bb, n, 0)),
        pl.BlockSpec((6, tcp, LANES), lambda l, bb, n: (0, n, 0)),
        pl.BlockSpec(dmask.shape, lambda l, bb, n: (0, 0, 0), pipeline_mode=const),
        pl.BlockSpec(rdec.shape, lambda l, bb, n: (0, 0, 0), pipeline_mode=const),
        sspec(kwin.shape, kwin.ndim == 4), sspec(vwin.shape, vwin.ndim == 4),
        sspec(sre0.shape, sre0.ndim == 3), sspec(sim0.shape, sim0.ndim == 3),
        sspec(r0.shape, r0.ndim == 4),
        pl.BlockSpec(memory_space=pltpu.SMEM),
        wspec(ng.shape), wspec(w_in.shape), wspec(lam.shape), wspec(bbd.shape), wspec(wc.shape),
        wspec(d_row.shape), wspec(wglu.shape), wspec(rg.shape), wspec(w_out.shape), wspec(w_pp.shape),
        wspec(w_gate.shape),
        pl.BlockSpec(fg.shape, lambda l, bb, n: (0, 0), pipeline_mode=const),
    ]
    out_shape = (
        jax.ShapeDtypeStruct((batch, t_pad, D_MODEL), F32),
        jax.ShapeDtypeStruct((n_layers, batch, WINDOW, LANES), F32),
        jax.ShapeDtypeStruct((n_layers, batch, WINDOW, LANES), F32),
        jax.ShapeDtypeStruct((n_layers, batch, D_STATE), F32),
        jax.ShapeDtypeStruct((n_layers, batch, D_STATE), F32),
        jax.ShapeDtypeStruct((n_layers, batch, N_RET_HEADS * HEAD_DIM, HEAD_DIM), F32),
    )
    out_specs = (
        pl.BlockSpec((bblk, tcp, D_MODEL), h_out_map),
        ospec((WINDOW, LANES)), ospec((WINDOW, LANES)), ospec((D_STATE,)), ospec((D_STATE,)),
        ospec((N_RET_HEADS * HEAD_DIM, HEAD_DIM)),
    )
    scratch = [
        pltpu.VMEM((bblk, 2 * WINDOW, LANES), F32),
        pltpu.VMEM((bblk, 2 * WINDOW, LANES), F32),
        pltpu.VMEM((bblk, D_STATE), F32),
        pltpu.VMEM((bblk, D_STATE), F32),
        pltpu.VMEM((bblk, N_RET_HEADS // ret_group, w_ret, w_ret), F32),
        pltpu.VMEM((D_SSM, 2 * D_STATE), BF16),
        pltpu.VMEM((2, D_STATE), F32),
        pltpu.VMEM((rows, D_MODEL), BF16),
        pltpu.VMEM((rgrp, COL_B - COL_A), F32),
        pltpu.VMEM((rgrp, D_IN - COL_C), F32),
        pltpu.VMEM((2, rows, LANES), F32),
        pltpu.VMEM((rows, D_SSM), F32),
        pltpu.VMEM((rows, D_ATTN), F32),
        pltpu.VMEM((rows, D_RET), F32),
        pltpu.VMEM((2, rows, LANES), F32),
        pltpu.VMEM((rows_sb, D_SSM), F32),
        pltpu.VMEM((rows_sb, 2 * D_STATE), F32),
        pltpu.VMEM((rgrp, D_MODEL), F32),
        pltpu.VMEM((rgrp, D_MODEL), BF16),
    ]
    if resident_h:
        scratch.append(pltpu.VMEM((batch, tcp, D_MODEL), F32))
    return pl.pallas_call(
        functools.partial(_layer_kernel, cfg),
        grid=(n_layers, nb, n_chunks),
        in_specs=in_specs,
        out_specs=out_specs,
        out_shape=out_shape,
        scratch_shapes=scratch,
        compiler_params=pltpu.CompilerParams(
            dimension_semantics=("arbitrary", "arbitrary", "arbitrary"),
            vmem_limit_bytes=VMEM_LIMIT),
        name="sample_layers" if resident_h else "prompt_layer_%d" % layer0,
    )(h, p_all, rope, dmask, rdec, kwin, vwin, sre0, sim0, r0,
      sink, ng, w_in, lam, bbd, wc, d_row, wglu, rg, w_out, w_pp, w_gate, fg)


def kernel(x_prompt, x_sample, p_prompt, p_sample, cache_win_k, cache_win_v, state_ssm_re, state_ssm_im,
           state_ret, norm_g, w_in, attn_sink, ssm_lam_re, ssm_lam_im, ssm_log_dt, ssm_b_re, ssm_b_im,
           ssm_c_re, ssm_c_im, ssm_d, w_glu, ret_norm_g, w_out, w_ple_proj, w_ple_gate, final_norm_g):
    depth = w_in.shape[0]
    bp, seq = x_prompt.shape[0], x_prompt.shape[1]
    bs, dec_seq = x_sample.shape[0], x_sample.shape[1]
    past_len = 8192

    same_group = jnp.asarray(np.arange(D_SSM)[:, None] // SSM_GROUP == np.arange(D_STATE)[None, :] // SSM_STATE)
    bbd = jnp.stack([
        jnp.where(same_group, jnp.tile(b.transpose(0, 1, 3, 2).reshape(depth, D_SSM, SSM_STATE),
                                       (1, 1, N_SSM_GROUPS)), 0.0)
        for b in (ssm_b_re, ssm_b_im)], axis=1)
    wc = jnp.stack([
        jnp.where(same_group.T, jnp.tile(c.transpose(0, 1, 3, 2).reshape(depth, D_STATE, SSM_GROUP),
                                         (1, 1, N_SSM_GROUPS)), 0.0).astype(BF16)
        for c in (ssm_c_re, ssm_c_im)], axis=1)
    lam = jnp.stack([ssm_lam_re.reshape(depth, D_STATE), ssm_lam_im.reshape(depth, D_STATE),
                     jnp.repeat(ssm_log_dt, SSM_STATE, axis=-1)], axis=1)
    weights = (
        attn_sink,
        norm_g.reshape(depth, 1, D_MODEL), w_in.astype(BF16), lam, bbd, wc,
        ssm_d.reshape(depth, 1, D_SSM), w_glu.astype(BF16), ret_norm_g.reshape(depth, 1, D_RET),
        w_out.astype(BF16), w_ple_proj.astype(BF16), w_ple_gate.astype(BF16),
        final_norm_g.reshape(1, D_MODEL),
    )

    tc_p = WINDOW
    rope_p = _rope_tables(0.0, seq)
    dmask_p, rdec_p = _retention_tables(tc_p, tc_p, GROUPS_PROMPT[1])
    zk = jnp.zeros((bp, WINDOW, LANES), F32)
    zs = jnp.zeros((bp, D_STATE), F32)
    zr = jnp.zeros((bp, N_RET_HEADS * HEAD_DIM, HEAD_DIM), F32)

    tcp_s = 8
    bblk_s = 16
    rope_s = _rope_tables(float(past_len), tcp_s)
    dmask_s, rdec_s = _retention_tables(dec_seq, tcp_s, GROUPS_SAMPLE[1])
    pad_t = ((0, 0), (0, tcp_s - dec_seq), (0, 0))
    hs = jnp.pad(x_sample, pad_t)
    ps = jnp.pad(p_sample, ((0, 0),) + pad_t)
    kwin_s = cache_win_k.reshape(depth, bs, WINDOW, LANES)
    vwin_s = cache_win_v.reshape(depth, bs, WINDOW, LANES)
    sre_s = state_ssm_re.reshape(depth, bs, D_STATE)
    sim_s = state_ssm_im.reshape(depth, bs, D_STATE)
    r_s = state_ret.reshape(depth, bs, N_RET_HEADS * HEAD_DIM, HEAD_DIM)

    hp = x_prompt
    outs_p = []
    for i in range(depth):
        cfg_p = (bp, tc_p, tc_p, seq // tc_p, 0, i, 1, i == depth - 1) + GROUPS_PROMPT
        hp, *st = _layer_call(cfg_p, bp, hp, p_prompt, rope_p, dmask_p, rdec_p, zk, zk, zs, zs, zr, weights)
        outs_p.append(st)
    st_p = [jnp.concatenate(t, axis=0) for t in zip(*outs_p)]
    cfg_s = (bblk_s, dec_seq, tcp_s, 1, past_len, 0, depth, True) + GROUPS_SAMPLE
    hs, *st_s = _layer_call(cfg_s, bs, hs, ps, rope_s, dmask_s, rdec_s, kwin_s, vwin_s, sre_s, sim_s, r_s, weights)

    def pack(st, b):
        k, v, sr, si, r = st
        return (k.reshape(depth, b, WINDOW, N_KV_HEADS, HEAD_DIM), v.reshape(depth, b, WINDOW, N_KV_HEADS, HEAD_DIM),
                sr.reshape(depth, b, N_SSM_GROUPS, SSM_STATE), si.reshape(depth, b, N_SSM_GROUPS, SSM_STATE),
                r.reshape(depth, b, N_RET_HEADS, HEAD_DIM, HEAD_DIM))

    return (hp, hs[:, :dec_seq]) + pack(st_p, bp) + pack(st_s, bs)
```

```python
import functools
import math

import numpy as np
import jax
import jax.numpy as jnp
from jax import lax
from jax.experimental import pallas as pl
from jax.experimental.pallas import tpu as pltpu

F32 = jnp.float32
BF16 = jnp.bfloat16

D_MODEL = 1024
HEAD_DIM = 64
D_ATTN = 512
N_Q_HEADS = 8
N_KV_HEADS = 2
Q_PER_KV = N_Q_HEADS // N_KV_HEADS
D_KV = 128
WINDOW = 128
ROPE_THETA = 500000.0
ROT_DIM = 16
D_SSM = 256
SSM_GROUP = 16
N_SSM_GROUPS = 16
SSM_STATE = 64
D_STATE = N_SSM_GROUPS * SSM_STATE
D_RET = 256
N_RET_HEADS = 4
RET_THETA = 10000.0
PLE_DIM = 256
EPS = 1e-6
LOG2E = math.log2(math.e)
D_IN = 2816
COL_A = 0
COL_B = 1280
COL_C = 1792
LANES = 128
MXU_N = 256
ROW_GROUP = 256
VMEM_LIMIT = 58 * 1024 * 1024
GROUPS_PROMPT = (2, 2)
GROUPS_SAMPLE = (8, 2)


def _mm(a, b):
    return jnp.dot(a.astype(BF16), b.astype(BF16), preferred_element_type=F32)


def _mm_nt(a, b):
    return lax.dot_general(a.astype(BF16), b.astype(BF16), (((1,), (1,)), ((), ())),
                           preferred_element_type=F32)


def _silu(x):
    return x * jax.nn.sigmoid(x)


def _layer_kernel(cfg,
                  h_ref, p_ref, rope_ref, dmask_ref, rdec_ref,
                  kwin_ref, vwin_ref, sre0_ref, sim0_ref, r0_ref,
                  sink_ref, ng_ref, win_ref, lam_ref, bbd_ref, wc_ref, d_ref, wglu_ref, rg_ref,
                  wout_ref, wpp_ref, wgate_ref, fg_ref,
                  hout_ref, kout_ref, vout_ref, sreo_ref, simo_ref, ro_ref,
                  kbuf, vbuf, sre, sim, rst, wb_s, a_s, hn_s, pa_s, pc_s, u_s, gb_s,
                  ya_s, yc_s, yb_s, ut_s, bu_s, h1_s, h1b_s, *resident):
    bblk, tc, tcp, n_chunks, pos0, layer0, n_layers, final_norm, att_group, ret_group = cfg
    w_ret = ret_group * HEAD_DIM
    n_rg = N_RET_HEADS // ret_group
    rgrp = min(ROW_GROUP, bblk * tcp)
    gsz = rgrp // tcp
    n_groups = bblk // gsz
    sb = min(tc, rgrp // bblk)
    assert tc // sb == n_groups
    assert tcp & (tcp - 1) == 0
    lyr = pl.program_id(0)
    n = pl.program_id(2)
    if resident:
        h_all, = resident
        hb = pl.multiple_of(pl.program_id(1) * bblk, bblk)

        @pl.when(lyr == 0)
        def _load_h():
            h_all[pl.ds(hb, bblk)] = h_ref[...]

        def h_rows(b0, cols=slice(None)):
            return h_all[pl.ds(hb + b0, gsz), :, cols]
    else:
        def h_rows(b0, cols=slice(None)):
            return h_ref[pl.ds(b0, gsz), :, cols]

    def head_of(shape, dim):
        return lax.broadcasted_iota(jnp.int32, shape, dim) >> (HEAD_DIM.bit_length() - 1)

    same_head = head_of((w_ret, w_ret), 0) == head_of((w_ret, w_ret), 1)

    @pl.when(n == 0)
    def _init():
        kbuf[:, 0:WINDOW] = kwin_ref[...]
        vbuf[:, 0:WINDOW] = vwin_ref[...]
        if tcp < WINDOW:
            kbuf[:, WINDOW:2 * WINDOW] = jnp.zeros((bblk, WINDOW, LANES), F32)
            vbuf[:, WINDOW:2 * WINDOW] = jnp.zeros((bblk, WINDOW, LANES), F32)
        sre[...] = sre0_ref[...]
        sim[...] = sim0_ref[...]
        r_in = r0_ref[...].reshape(bblk, n_rg, w_ret, HEAD_DIM)
        rst[...] = jnp.where(same_head, jnp.concatenate([r_in] * ret_group, axis=-1), 0.0)
        if tc < tcp:
            yb_s[...] = jnp.zeros_like(yb_s)

    @pl.when((n == 0) & (pl.program_id(1) == 0))
    def _discretise():
        lr = lam_ref[0:1, :]
        li = lam_ref[1:2, :]
        dt = jnp.exp(lam_ref[2:3, :])
        mag = jnp.exp(lr * dt)
        a_re = mag * jnp.cos(li * dt)
        a_im = mag * jnp.sin(li * dt)
        nr = a_re - 1.0
        ni = a_im
        den = lr * lr + li * li
        coef_re = (nr * lr + ni * li) / den
        coef_im = (ni * lr - nr * li) / den
        a_s[0:1, :] = a_re
        a_s[1:2, :] = a_im
        br = bbd_ref[0]
        bi = bbd_ref[1]
        wb_s[:, 0:D_STATE] = (coef_re * br - coef_im * bi).astype(BF16)
        wb_s[:, D_STATE:2 * D_STATE] = (coef_re * bi + coef_im * br).astype(BF16)

    lo_q = head_of((tcp, LANES), 1) == 0
    headmean = jnp.where(same_head, 1.0 / HEAD_DIM, 0.0).astype(BF16)
    headmean2 = jnp.concatenate([headmean, headmean], axis=0)
    ones_k = jnp.ones((2 * WINDOW, LANES), BF16)
    ret_lane_head = head_of((tcp, w_ret), 1)

    cos_a, sa_hi, sa_lo = rope_ref[0], rope_ref[1], rope_ref[2]
    cos_r, sr_hi, sr_lo = rope_ref[3], rope_ref[4], rope_ref[5]

    def rope_a(x):
        return (x * cos_a + pltpu.roll(x, ROT_DIM // 2, 1) * sa_hi
                + pltpu.roll(x, LANES - ROT_DIM // 2, 1) * sa_lo)

    def rope_r(x):
        return (x * cos_r + pltpu.roll(x, HEAD_DIM // 2, 1) * sr_hi
                + pltpu.roll(x, LANES - HEAD_DIM // 2, 1) * sr_lo)

    lmin = WINDOW - pos0 - n * tc

    def valid_keys(n_heads):
        qi = lax.broadcasted_iota(jnp.int32, (n_heads * tcp, 2 * WINDOW), 0) & (tcp - 1)
        kl = lax.broadcasted_iota(jnp.int32, (n_heads * tcp, 2 * WINDOW), 1)
        return (kl > qi) & (kl <= qi + WINDOW) & (kl >= lmin)

    valid = valid_keys(1)
    valid2 = valid_keys(2)
    first_head = lax.broadcasted_iota(jnp.int32, (2 * tcp, 1), 0) < tcp
    lo_k = head_of((2 * WINDOW, LANES), 1) == 0

    def norm_body(g, carry):
        b0 = pl.multiple_of(g * gsz, gsz)
        r0 = pl.multiple_of(g * rgrp, rgrp)
        x = h_rows(b0).reshape(rgrp, D_MODEL)
        ms = jnp.mean(x * x, axis=-1, keepdims=True)
        hn = (x * lax.rsqrt(ms + EPS) * ng_ref[...]).astype(BF16)
        hn_s[pl.ds(r0, rgrp), :] = hn
        ub = jnp.dot(hn, win_ref[:, COL_B:COL_B + D_SSM], preferred_element_type=F32)
        u_s[0, pl.ds(r0, rgrp), :] = ub[:, 0:LANES]
        u_s[1, pl.ds(r0, rgrp), :] = ub[:, LANES:2 * LANES]
        gb_s[pl.ds(r0, rgrp), :] = jnp.dot(hn, win_ref[:, COL_B + D_SSM:COL_C],
                                                preferred_element_type=F32)
        return carry

    lax.fori_loop(0, n_groups, norm_body, 0, unroll=True)

    a_re_b = jnp.broadcast_to(a_s[0:1, :], (bblk, D_STATE))
    a_im_b = jnp.broadcast_to(a_s[1:2, :], (bblk, D_STATE))

    def s5_sub_block(si):
        t0 = si * sb
        for t in range(sb):
            for s in range(2):
                ut_s[t * bblk:(t + 1) * bblk, s * LANES:(s + 1) * LANES] = (
                    u_s[s, pl.ds(t0 + t, bblk, stride=tcp), :])
        for c in range(0, 2 * D_STATE, MXU_N):
            bu_s[:, c:c + MXU_N] = _mm(ut_s[...], wb_s[:, c:c + MXU_N])

        def step(t, xs):
            xr, xi = xs
            rt = pl.multiple_of(t * bblk, bblk)
            b_re = bu_s[pl.ds(rt, bblk), 0:D_STATE]
            b_im = bu_s[pl.ds(rt, bblk), D_STATE:2 * D_STATE]
            nxr = a_re_b * xr - a_im_b * xi + b_re
            nxi = a_re_b * xi + a_im_b * xr + b_im
            bu_s[pl.ds(rt, bblk), 0:D_STATE] = nxr
            bu_s[pl.ds(rt, bblk), D_STATE:2 * D_STATE] = nxi
            return nxr, nxi

        xr, xi = lax.fori_loop(0, sb, step, (sre[...], sim[...]), unroll=True)
        sre[...] = xr
        sim[...] = xi
        y = (_mm(bu_s[:, 0:D_STATE], wc_ref[0]) - _mm(bu_s[:, D_STATE:2 * D_STATE], wc_ref[1])
             + d_ref[...] * ut_s[...])
        y = jax.nn.gelu(y)
        y = y * jax.nn.sigmoid(_mm(y, wglu_ref[...]))
        for t in range(sb):
            for s in range(2):
                yb_s[s, pl.ds(t0 + t, bblk, stride=tcp), :] = y[t * bblk:(t + 1) * bblk, s * LANES:(s + 1) * LANES]

    def group_body(g, carry):
        b0 = pl.multiple_of(g * gsz, gsz)
        r0 = pl.multiple_of(g * rgrp, rgrp)
        hn = hn_s[pl.ds(r0, rgrp), :]
        for c in range(0, COL_B - COL_A, MXU_N):
            pa_s[:, c:c + MXU_N] = jnp.dot(hn, win_ref[:, COL_A + c:COL_A + c + MXU_N],
                                           preferred_element_type=F32)
        for c in range(0, D_IN - COL_C, MXU_N):
            pc_s[:, c:c + MXU_N] = jnp.dot(hn, win_ref[:, COL_C + c:COL_C + c + MXU_N],
                                           preferred_element_type=F32)

        def batch_body(j, carry2):
            b = b0 + j
            rj = pl.multiple_of(j * tcp, tcp)
            rb = pl.multiple_of(b * tcp, tcp)
            k_cur = rope_a(pa_s[pl.ds(rj, tcp), D_ATTN:D_ATTN + D_KV])
            v_cur = pa_s[pl.ds(rj, tcp), D_ATTN + D_KV:D_ATTN + 2 * D_KV]
            if tcp == WINDOW:
                kout_ref[b] = k_cur
                vout_ref[b] = v_cur
            else:
                keep = lax.broadcasted_iota(jnp.int32, (tcp, LANES), 0) < tcp - tc
                for buf, cur, out in ((kbuf, k_cur, kout_ref), (vbuf, v_cur, vout_ref)):
                    shifted = pltpu.roll(buf[b, 0:WINDOW], WINDOW - tc, 0)
                    out[b, 0:WINDOW - tcp] = shifted[0:WINDOW - tcp]
                    out[b, WINDOW - tcp:WINDOW] = jnp.where(keep, shifted[WINDOW - tcp:WINDOW],
                                                            pltpu.roll(cur, tcp - tc, 0))
            kbuf[b, WINDOW:WINDOW + tcp] = k_cur
            vbuf[b, WINDOW:WINDOW + tcp] = v_cur
            def q_pair(pj):
                return rope_a(pa_s[pl.ds(rj, tcp), pj * LANES:(pj + 1) * LANES])

            def softmax_matmul(q_grp, heads, k_op, v_op):
                s_grp = _mm_nt(q_grp * (HEAD_DIM ** -0.5 * LOG2E), k_op)
                ex, sink_w = [], []
                for t, hd in enumerate(heads):
                    s = jnp.where(valid, s_grp[t * tcp:(t + 1) * tcp], -jnp.inf)
                    sk = sink_ref[layer0 + lyr, hd] * LOG2E
                    m = jnp.maximum(jnp.max(s, axis=-1, keepdims=True), sk)
                    ex.append(jnp.exp2(s - m))
                    sink_w.append(jnp.exp2(sk - m))
                r = _mm(jnp.concatenate(ex, axis=0), jnp.concatenate([v_op.astype(BF16), ones_k], axis=1))
                o_grp = r[:, 0:LANES] / (r[:, LANES:2 * LANES] + jnp.concatenate(sink_w, axis=0))
                return [o_grp[t * tcp:(t + 1) * tcp] for t in range(len(heads))]

            k_cat, v_cat = kbuf[b], vbuf[b]
            hi_q = jnp.logical_not(lo_q)
            o_heads = [None] * N_Q_HEADS
            if att_group == N_Q_HEADS:
                q_rows = []
                for pj in range(N_Q_HEADS // 2):
                    qp = q_pair(pj)
                    qx = pltpu.roll(qp, HEAD_DIM, 1)
                    kv_lo = (pj // 2 == 0)
                    q_even, q_odd = (qp, qx) if kv_lo else (qx, qp)
                    keep = lo_q if kv_lo else hi_q
                    q_rows += [jnp.where(keep, q_even, 0.0), jnp.where(keep, q_odd, 0.0)]
                o_heads = softmax_matmul(jnp.concatenate(q_rows, axis=0), range(N_Q_HEADS), k_cat, v_cat)
                for hd in range(N_Q_HEADS):
                    if (hd % 2 == 0) != (hd // Q_PER_KV == 0):
                        o_heads[hd] = pltpu.roll(o_heads[hd], HEAD_DIM, 1)
            else:
                k_rol, v_rol = pltpu.roll(k_cat, HEAD_DIM, 1), pltpu.roll(v_cat, HEAD_DIM, 1)
                for kv in range(N_KV_HEADS):
                    qs = jnp.concatenate([q_pair(2 * kv), q_pair(2 * kv + 1)], axis=0)
                    qs = (qs * (HEAD_DIM ** -0.5 * LOG2E)).astype(BF16)
                    o = jnp.zeros((2 * tcp, LANES), F32)
                    for e in range(2):
                        keep = lo_k if e == 0 else jnp.logical_not(lo_k)
                        k_op = jnp.where(keep, k_cat if e == kv else k_rol, 0.0)
                        v_op = jnp.where(keep, v_cat if e == kv else v_rol, 0.0)
                        s = jnp.where(valid2, _mm_nt(qs, k_op), -jnp.inf)
                        sk = jnp.where(first_head, sink_ref[layer0 + lyr, 4 * kv + e],
                                       sink_ref[layer0 + lyr, 4 * kv + 2 + e]) * LOG2E
                        m = jnp.maximum(jnp.max(s, axis=-1, keepdims=True), sk)
                        r = _mm(jnp.exp2(s - m), jnp.concatenate([v_op.astype(BF16), ones_k], axis=1))
                        o = o + r[:, 0:LANES] / (r[:, LANES:2 * LANES] + jnp.exp2(sk - m))
                    o_heads[4 * kv:4 * kv + 4] = [o[0:tcp], o[0:tcp], o[tcp:2 * tcp], o[tcp:2 * tcp]]
            for pj in range(N_Q_HEADS // 2):
                ga = pa_s[pl.ds(rj, tcp),
                          D_ATTN + 2 * D_KV + pj * LANES:D_ATTN + 2 * D_KV + (pj + 1) * LANES]
                ya_s[pl.ds(rb, tcp), pj * LANES:(pj + 1) * LANES] = (
                    jnp.where(lo_q, o_heads[2 * pj], o_heads[2 * pj + 1]) * _silu(ga))
            if n_chunks > 1:
                kbuf[b, 0:WINDOW] = k_cur
                vbuf[b, 0:WINDOW] = v_cur
            for pr in range(n_rg):
                cs = slice(pr * w_ret, (pr + 1) * w_ret)

                def rope_cols(c0):
                    return jnp.concatenate([rope_r(pc_s[pl.ds(rj, tcp), c0 + t * LANES:c0 + (t + 1) * LANES])
                                            for t in range(w_ret // LANES)], axis=1)

                q = rope_cols(pr * w_ret)
                k = rope_cols(D_RET + pr * w_ret) * (HEAD_DIM ** -0.5)
                v = pc_s[pl.ds(rj, tcp), 2 * D_RET + pr * w_ret:2 * D_RET + (pr + 1) * w_ret]
                gr = pc_s[pl.ds(rj, tcp), 3 * D_RET + pr * w_ret:3 * D_RET + (pr + 1) * w_ret]
                r_old = rst[b, pr]
                k_heads = jnp.concatenate([jnp.where(ret_lane_head == hd, k, 0.0) for hd in range(ret_group)], axis=0)
                v_heads = jnp.concatenate([jnp.where(ret_lane_head == hd, v, 0.0) for hd in range(ret_group)], axis=0)
                sc = _mm_nt(q, k_heads) * dmask_ref[pr]
                o = _mm(q, r_old) * rdec_ref[1, :, cs] + _mm(sc, v_heads)
                kd = k * rdec_ref[0, :, cs]
                if tcp < WINDOW:
                    pad = jnp.zeros((WINDOW - tcp, w_ret), F32)
                    d_new = _mm(jnp.concatenate([kd, pad], axis=0).T, jnp.concatenate([v, pad], axis=0))
                else:
                    d_new = _mm(kd.T, v)
                rst[b, pr] = r_old * rdec_ref[2, 0:1, cs] + jnp.where(same_head, d_new, 0.0)
                o2 = o * o
                o2_hi = o2.astype(BF16)
                o2_lo = (o2 - o2_hi.astype(F32)).astype(BF16)
                msq = jnp.dot(jnp.concatenate([o2_hi, o2_lo], axis=1), headmean2, preferred_element_type=F32)
                yc_s[pl.ds(rb, tcp), cs] = o * lax.rsqrt(msq + EPS) * rg_ref[:, cs] * _silu(gr)
            return carry2

        lax.fori_loop(0, gsz, batch_body, 0, unroll=True)
        s5_sub_block(g)
        return carry

    lax.fori_loop(0, n_groups, group_body, 0, unroll=min(n_groups, 4))

    def out_body(g, carry):
        b0 = pl.multiple_of(g * gsz, gsz)
        r0 = pl.multiple_of(g * rgrp, rgrp)
        ya = ya_s[pl.ds(r0, rgrp), :].astype(BF16)
        gate_b = _silu(gb_s[pl.ds(r0, rgrp), :])
        yb0 = (yb_s[0, pl.ds(r0, rgrp), :] * gate_b[:, 0:LANES]).astype(BF16)
        yb1 = (yb_s[1, pl.ds(r0, rgrp), :] * gate_b[:, LANES:2 * LANES]).astype(BF16)
        yc = yc_s[pl.ds(r0, rgrp), :].astype(BF16)
        for c in range(0, D_MODEL, MXU_N):
            cc = slice(c, c + MXU_N)
            y = (jnp.dot(ya, wout_ref[0:D_ATTN, cc], preferred_element_type=F32)
                 + jnp.dot(yb0, wout_ref[D_ATTN:D_ATTN + LANES, cc], preferred_element_type=F32)
                 + jnp.dot(yb1, wout_ref[D_ATTN + LANES:D_ATTN + D_SSM, cc], preferred_element_type=F32)
                 + jnp.dot(yc, wout_ref[D_ATTN + D_SSM:D_MODEL, cc], preferred_element_type=F32))
            h1 = h_rows(b0, cc).reshape(rgrp, MXU_N) + y
            h1_s[:, cc] = h1
            h1b_s[:, cc] = h1.astype(BF16)
        pb = p_ref[pl.ds(b0, gsz)].reshape(rgrp, PLE_DIM).astype(BF16)
        ssq = jnp.zeros((rgrp, 1), F32)
        for c in range(0, D_MODEL, MXU_N):
            cc = slice(c, c + MXU_N)
            gate = jax.nn.sigmoid(jnp.dot(h1b_s[...], wgate_ref[:, cc], preferred_element_type=F32))
            pp = jnp.dot(pb, wpp_ref[:, cc], preferred_element_type=F32)
            h2 = h1_s[:, cc] + gate * pp
            if final_norm:
                ssq = ssq + jnp.sum(h2 * h2, axis=-1, keepdims=True)
                h1_s[:, cc] = h2
            else:
                hout_ref[pl.ds(b0, gsz), :, cc] = h2.reshape(gsz, tcp, MXU_N)

        def write_normed():
            scale = lax.rsqrt(ssq * (1.0 / D_MODEL) + EPS)
            hout_ref[pl.ds(b0, gsz)] = (h1_s[...] * scale * fg_ref[...]).reshape(gsz, tcp, D_MODEL)

        if resident:
            h_all[pl.ds(hb + b0, gsz)] = h1_s[...].reshape(gsz, tcp, D_MODEL)
            pl.when(lyr == n_layers - 1)(write_normed)
        elif final_norm:
            write_normed()
        return carry

    lax.fori_loop(0, n_groups, out_body, 0, unroll=min(n_groups, 2))

    @pl.when(n == n_chunks - 1)
    def _finish():
        sreo_ref[...] = sre[...]
        simo_ref[...] = sim[...]
        r_fin = rst[...]
        ro_ref[...] = sum(r_fin[..., hd * HEAD_DIM:(hd + 1) * HEAD_DIM] for hd in range(ret_group)).reshape(
            bblk, N_RET_HEADS * HEAD_DIM, HEAD_DIM)


def _rope_tables(pos0, t_pad):
    pos = jnp.arange(t_pad, dtype=F32) + pos0
    lane = np.arange(LANES) % HEAD_DIM

    def tables(theta, rot_dim):
        half = rot_dim // 2
        inv = 1.0 / jnp.power(theta, jnp.arange(half, dtype=F32) / half)
        ang = pos[:, None] * inv[None, :]
        cos = jnp.cos(ang)
        sin = jnp.sin(ang)
        idx = np.where(lane < rot_dim, lane % half, 0)
        in_rot = jnp.asarray(lane < rot_dim)
        is_hi = jnp.asarray((lane >= half) & (lane < rot_dim))
        is_lo = jnp.asarray(lane < half)
        cos_l = jnp.where(in_rot[None, :], cos[:, idx], 1.0)
        sin_l = sin[:, idx]
        return (cos_l, jnp.where(is_hi[None, :], sin_l, 0.0), jnp.where(is_lo[None, :], -sin_l, 0.0))

    return jnp.stack(tables(ROPE_THETA, ROT_DIM) + tables(RET_THETA, HEAD_DIM))


def _retention_tables(tc, tcp, ret_group):
    gammas = 1.0 - jnp.power(2.0, -5.0 - jnp.arange(N_RET_HEADS, dtype=F32))
    log_g = jnp.log(gammas)
    i = jnp.arange(tcp, dtype=F32)
    rel = i[:, None] - i[None, :]
    dmask = jnp.where(rel[None] >= 0, jnp.exp(log_g[:, None, None] * jnp.maximum(rel, 0.0)[None]), 0.0)
    kdec = jnp.exp(log_g[None, :] * (tc - 1.0 - i)[:, None])
    qdec = jnp.exp(log_g[None, :] * (i + 1.0)[:, None])
    gch = jnp.broadcast_to(jnp.exp(log_g * tc)[None, :], (tcp, N_RET_HEADS))
    rdec = jnp.stack([jnp.repeat(t, HEAD_DIM, axis=1) for t in (kdec, qdec, gch)])
    dmask = dmask.reshape(N_RET_HEADS // ret_group, ret_group, tcp, tcp).transpose(0, 2, 1, 3).reshape(
        N_RET_HEADS // ret_group, tcp, ret_group * tcp)
    return dmask, rdec


def _layer_call(cfg, batch, h, p_all, rope, dmask, rdec, kwin, vwin, sre0, sim0, r0, weights):
    bblk, tc, tcp, n_chunks, pos0, layer0, n_layers, final_norm, att_group, ret_group = cfg
    w_ret = ret_group * HEAD_DIM
    resident_h = n_layers > 1
    assert not resident_h or (n_chunks == 1 and final_norm)
    rows = bblk * tcp
    rgrp = min(ROW_GROUP, rows)
    rows_sb = min(tc, rgrp // bblk) * bblk
    nb = batch // bblk
    t_pad = n_chunks * tcp
    const = pl.Buffered(1)

    def wspec(shape):
        nd = len(shape)
        return pl.BlockSpec((None,) + tuple(shape[1:]), lambda l, bb, n: (layer0 + l,) + (0,) * (nd - 1),
                            pipeline_mode=const)

    def sspec(shape, lead_layer, mode=None):
        nd = len(shape)
        if lead_layer:
            return pl.BlockSpec((None, bblk) + tuple(shape[2:]),
                                lambda l, bb, n: (layer0 + l, bb) + (0,) * (nd - 2), pipeline_mode=mode)
        return pl.BlockSpec((bblk,) + tuple(shape[1:]), lambda l, bb, n: (bb,) + (0,) * (nd - 1),
                            pipeline_mode=mode)

    big_in_mode = const if resident_h else None

    def ospec(tail):
        return pl.BlockSpec((None, bblk) + tail, lambda l, bb, n: (l, bb) + (0,) * len(tail))

    if resident_h:
        h_in_map = lambda l, bb, n: (jnp.where(l == 0, bb, 0), 0, 0)
        h_out_map = lambda l, bb, n: (jnp.where(l == n_layers - 1, bb, 0), 0, 0)
    else:
        h_in_map = h_out_map = lambda l, bb, n: (bb, n, 0)

    (sink, ng, w_in, lam, bbd, wc, d_row, wglu, rg, w_out, w_pp, w_gate, fg) = weights
    in_specs = [
        pl.BlockSpec((bblk, tcp, D_MODEL), h_in_map, pipeline_mode=big_in_mode),
        pl.BlockSpec((None, bblk, tcp, PLE_DIM), lambda l, bb, n: (layer0 + l, bb, n, 0)),
        pl.BlockSpec((6, tcp, LANES), lambda l, bb, n: (0, n, 0)),
        pl.BlockSpec(dmask.shape, lambda l, bb, n: (0, 0, 0), pipeline_mode=const),
        pl.BlockSpec(rdec.shape, lambda l, bb, n: (0, 0, 0), pipeline_mode=const),
        sspec(kwin.shape, kwin.ndim == 4), sspec(vwin.shape, vwin.ndim == 4),
        sspec(sre0.shape, sre0.ndim == 3), sspec(sim0.shape, sim0.ndim == 3),
        sspec(r0.shape, r0.ndim == 4),
        pl.BlockSpec(memory_space=pltpu.SMEM),
        wspec(ng.shape), wspec(w_in.shape), wspec(lam.shape), wspec(bbd.shape), wspec(wc.shape),
        wspec(d_row.shape), wspec(wglu.shape), wspec(rg.shape), wspec(w_out.shape), wspec(w_pp.shape),
        wspec(w_gate.shape),
        pl.BlockSpec(fg.shape, lambda l, bb, n: (0, 0), pipeline_mode=const),
    ]
    out_shape = (
        jax.ShapeDtypeStruct((batch, t_pad, D_MODEL), F32),
        jax.ShapeDtypeStruct((n_layers, batch, WINDOW, LANES), F32),
        jax.ShapeDtypeStruct((n_layers, batch, WINDOW, LANES), F32),
        jax.ShapeDtypeStruct((n_layers, batch, D_STATE), F32),
        jax.ShapeDtypeStruct((n_layers, batch, D_STATE), F32),
        jax.ShapeDtypeStruct((n_layers, batch, N_RET_HEADS * HEAD_DIM, HEAD_DIM), F32),
    )
    out_specs = (
        pl.BlockSpec((bblk, tcp, D_MODEL), h_out_map),
        ospec((WINDOW, LANES)), ospec((WINDOW, LANES)), ospec((D_STATE,)), ospec((D_STATE,)),
        ospec((N_RET_HEADS * HEAD_DIM, HEAD_DIM)),
    )
    scratch = [
        pltpu.VMEM((bblk, 2 * WINDOW, LANES), F32),
        pltpu.VMEM((bblk, 2 * WINDOW, LANES), F32),
        pltpu.VMEM((bblk, D_STATE), F32),
        pltpu.VMEM((bblk, D_STATE), F32),
        pltpu.VMEM((bblk, N_RET_HEADS // ret_group, w_ret, w_ret), F32),
        pltpu.VMEM((D_SSM, 2 * D_STATE), BF16),
        pltpu.VMEM((2, D_STATE), F32),
        pltpu.VMEM((rows, D_MODEL), BF16),
        pltpu.VMEM((rgrp, COL_B - COL_A), F32),
        pltpu.VMEM((rgrp, D_IN - COL_C), F32),
        pltpu.VMEM((2, rows, LANES), F32),
        pltpu.VMEM((rows, D_SSM), F32),
        pltpu.VMEM((rows, D_ATTN), F32),
        pltpu.VMEM((rows, D_RET), F32),
        pltpu.VMEM((2, rows, LANES), F32),
        pltpu.VMEM((rows_sb, D_SSM), F32),
        pltpu.VMEM((rows_sb, 2 * D_STATE), F32),
        pltpu.VMEM((rgrp, D_MODEL), F32),
        pltpu.VMEM((rgrp, D_MODEL), BF16),
    ]
    if resident_h:
        scratch.append(pltpu.VMEM((batch, tcp, D_MODEL), F32))
    return pl.pallas_call(
        functools.partial(_layer_kernel, cfg),
        grid=(n_layers, nb, n_chunks),
        in_specs=in_specs,
        out_specs=out_specs,
        out_shape=out_shape,
        scratch_shapes=scratch,
        compiler_params=pltpu.CompilerParams(
            dimension_semantics=("arbitrary", "arbitrary", "arbitrary"),
            vmem_limit_bytes=VMEM_LIMIT),
        name="sample_layers" if resident_h else "prompt_layer_%d" % layer0,
    )(h, p_all, rope, dmask, rdec, kwin, vwin, sre0, sim0, r0,
      sink, ng, w_in, lam, bbd, wc, d_row, wglu, rg, w_out, w_pp, w_gate, fg)


def kernel(x_prompt, x_sample, p_prompt, p_sample, cache_win_k, cache_win_v, state_ssm_re, state_ssm_im,
           state_ret, norm_g, w_in, attn_sink, ssm_lam_re, ssm_lam_im, ssm_log_dt, ssm_b_re, ssm_b_im,
           ssm_c_re, ssm_c_im, ssm_d, w_glu, ret_norm_g, w_out, w_ple_proj, w_ple_gate, final_norm_g):
    depth = w_in.shape[0]
    bp, seq = x_prompt.shape[0], x_prompt.shape[1]
    bs, dec_seq = x_sample.shape[0], x_sample.shape[1]
    past_len = 8192

    same_group = jnp.asarray(np.arange(D_SSM)[:, None] // SSM_GROUP == np.arange(D_STATE)[None, :] // SSM_STATE)
    bbd = jnp.stack([
        jnp.where(same_group, jnp.tile(b.transpose(0, 1, 3, 2).reshape(depth, D_SSM, SSM_STATE),
                                       (1, 1, N_SSM_GROUPS)), 0.0)
        for b in (ssm_b_re, ssm_b_im)], axis=1)
    wc = jnp.stack([
        jnp.where(same_group.T, jnp.tile(c.transpose(0, 1, 3, 2).reshape(depth, D_STATE, SSM_GROUP),
                                         (1, 1, N_SSM_GROUPS)), 0.0).astype(BF16)
        for c in (ssm_c_re, ssm_c_im)], axis=1)
    lam = jnp.stack([ssm_lam_re.reshape(depth, D_STATE), ssm_lam_im.reshape(depth, D_STATE),
                     jnp.repeat(ssm_log_dt, SSM_STATE, axis=-1)], axis=1)
    weights = (
        attn_sink,
        norm_g.reshape(depth, 1, D_MODEL), w_in.astype(BF16), lam, bbd, wc,
        ssm_d.reshape(depth, 1, D_SSM), w_glu.astype(BF16), ret_norm_g.reshape(depth, 1, D_RET),
        w_out.astype(BF16), w_ple_proj.astype(BF16), w_ple_gate.astype(BF16),
        final_norm_g.reshape(1, D_MODEL),
    )

    tc_p = WINDOW
    rope_p = _rope_tables(0.0, seq)
    dmask_p, rdec_p = _retention_tables(tc_p, tc_p, GROUPS_PROMPT[1])
    zk = jnp.zeros((bp, WINDOW, LANES), F32)
    zs = jnp.zeros((bp, D_STATE), F32)
    zr = jnp.zeros((bp, N_RET_HEADS * HEAD_DIM, HEAD_DIM), F32)

    tcp_s = 8
    bblk_s = 16
    rope_s = _rope_tables(float(past_len), tcp_s)
    dmask_s, rdec_s = _retention_tables(dec_seq, tcp_s, GROUPS_SAMPLE[1])
    pad_t = ((0, 0), (0, tcp_s - dec_seq), (0, 0))
    hs = jnp.pad(x_sample, pad_t)
    ps = jnp.pad(p_sample, ((0, 0),) + pad_t)
    kwin_s = cache_win_k.reshape(depth, bs, WINDOW, LANES)
    vwin_s = cache_win_v.reshape(depth, bs, WINDOW, LANES)
    sre_s = state_ssm_re.reshape(depth, bs, D_STATE)
    sim_s = state_ssm_im.reshape(depth, bs, D_STATE)
    r_s = state_ret.reshape(depth, bs, N_RET_HEADS * HEAD_DIM, HEAD_DIM)

    hp = x_prompt
    outs_p = []
    for i in range(depth):
        cfg_p = (bp, tc_p, tc_p, seq // tc_p, 0, i, 1, i == depth - 1) + GROUPS_PROMPT
        hp, *st = _layer_call(cfg_p, bp, hp, p_prompt, rope_p, dmask_p, rdec_p, zk, zk, zs, zs, zr, weights)
        outs_p.append(st)
    st_p = [jnp.concatenate(t, axis=0) for t in zip(*outs_p)]
    cfg_s = (bblk_s, dec_seq, tcp_s, 1, past_len, 0, depth, True) + GROUPS_SAMPLE
    hs, *st_s = _layer_call(cfg_s, bs, hs, ps, rope_s, dmask_s, rdec_s, kwin_s, vwin_s, sre_s, sim_s, r_s, weights)

    def pack(st, b):
        k, v, sr, si, r = st
        return (k.reshape(depth, b, WINDOW, N_KV_HEADS, HEAD_DIM), v.reshape(depth, b, WINDOW, N_KV_HEADS, HEAD_DIM),
                sr.reshape(depth, b, N_SSM_GROUPS, SSM_STATE), si.reshape(depth, b, N_SSM_GROUPS, SSM_STATE),
                r.reshape(depth, b, N_RET_HEADS, HEAD_DIM, HEAD_DIM))

    return (hp, hs[:, :dec_seq]) + pack(st_p, bp) + pack(st_s, bs)
```

```python
import functools
import math

import numpy as np
import jax
import jax.numpy as jnp
from jax import lax
from jax.experimental import pallas as pl
from jax.experimental.pallas import tpu as pltpu

F32 = jnp.float32
BF16 = jnp.bfloat16

D_MODEL = 1024
HEAD_DIM = 64
D_ATTN = 512
N_Q_HEADS = 8
N_KV_HEADS = 2
Q_PER_KV = N_Q_HEADS // N_KV_HEADS
D_KV = 128
WINDOW = 128
ROPE_THETA = 500000.0
ROT_DIM = 16
D_SSM = 256
SSM_GROUP = 16
N_SSM_GROUPS = 16
SSM_STATE = 64
D_STATE = N_SSM_GROUPS * SSM_STATE
D_RET = 256
N_RET_HEADS = 4
RET_THETA = 10000.0
PLE_DIM = 256
EPS = 1e-6
LOG2E = math.log2(math.e)
D_IN = 2816
COL_A = 0
COL_B = 1280
COL_C = 1792
LANES = 128
MXU_N = 256
ROW_GROUP = 256
VMEM_LIMIT = 58 * 1024 * 1024
GROUPS_PROMPT = (2, 2)
GROUPS_SAMPLE = (8, 2)


def _mm(a, b):
    return jnp.dot(a.astype(BF16), b.astype(BF16), preferred_element_type=F32)


def _mm_nt(a, b):
    return lax.dot_general(a.astype(BF16), b.astype(BF16), (((1,), (1,)), ((), ())),
                           preferred_element_type=F32)


def _silu(x):
    return x * jax.nn.sigmoid(x)


def _layer_kernel(cfg,
                  h_ref, p_ref, rope_ref, dmask_ref, rdec_ref,
                  kwin_ref, vwin_ref, sre0_ref, sim0_ref, r0_ref,
                  sink_ref, ng_ref, win_ref, lam_ref, bbd_ref, wc_ref, d_ref, wglu_ref, rg_ref,
                  wout_ref, wpp_ref, wgate_ref, fg_ref,
                  hout_ref, kout_ref, vout_ref, sreo_ref, simo_ref, ro_ref,
                  kbuf, vbuf, sre, sim, rst, wb_s, a_s, hn_s, pa_s, pc_s, u_s, gb_s,
                  ya_s, yc_s, yb_s, ut_s, bu_s, h1_s, h1b_s, *resident):
    bblk, tc, tcp, n_chunks, pos0, layer0, n_layers, final_norm, att_group, ret_group = cfg
    w_ret = ret_group * HEAD_DIM
    n_rg = N_RET_HEADS // ret_group
    rgrp = min(ROW_GROUP, bblk * tcp)
    gsz = rgrp // tcp
    n_groups = bblk // gsz
    sb = min(tc, rgrp // bblk)
    assert tc // sb == n_groups
    assert tcp & (tcp - 1) == 0
    cache_t = tcp < WINDOW
    assert not cache_t or (n_chunks == 1 and att_group == N_Q_HEADS)
    lyr = pl.program_id(0)
    n = pl.program_id(2)
    if resident:
        h_all, = resident
        hb = pl.multiple_of(pl.program_id(1) * bblk, bblk)

        @pl.when(lyr == 0)
        def _load_h():
            h_all[pl.ds(hb, bblk)] = h_ref[...]

        def h_rows(b0, cols=slice(None)):
            return h_all[pl.ds(hb + b0, gsz), :, cols]
    else:
        def h_rows(b0, cols=slice(None)):
            return h_ref[pl.ds(b0, gsz), :, cols]

    def head_of(shape, dim):
        return lax.broadcasted_iota(jnp.int32, shape, dim) >> (HEAD_DIM.bit_length() - 1)

    same_head = head_of((w_ret, w_ret), 0) == head_of((w_ret, w_ret), 1)

    @pl.when(n == 0)
    def _init():
        if cache_t:
            kbuf[:, :, 0:WINDOW] = kwin_ref[...]
            vbuf[:, :, 0:WINDOW] = vwin_ref[...]
        else:
            kbuf[:, 0:WINDOW] = kwin_ref[...]
            vbuf[:, 0:WINDOW] = vwin_ref[...]
        sre[...] = sre0_ref[...]
        sim[...] = sim0_ref[...]
        r_in = r0_ref[...].reshape(bblk, n_rg, w_ret, HEAD_DIM)
        rst[...] = jnp.where(same_head, jnp.concatenate([r_in] * ret_group, axis=-1), 0.0)
        if tc < tcp:
            yb_s[...] = jnp.zeros_like(yb_s)

    @pl.when((n == 0) & (pl.program_id(1) == 0))
    def _discretise():
        lr = lam_ref[0:1, :]
        li = lam_ref[1:2, :]
        dt = jnp.exp(lam_ref[2:3, :])
        mag = jnp.exp(lr * dt)
        a_re = mag * jnp.cos(li * dt)
        a_im = mag * jnp.sin(li * dt)
        nr = a_re - 1.0
        ni = a_im
        den = lr * lr + li * li
        coef_re = (nr * lr + ni * li) / den
        coef_im = (ni * lr - nr * li) / den
        a_s[0:1, :] = a_re
        a_s[1:2, :] = a_im
        br = bbd_ref[0]
        bi = bbd_ref[1]
        wb_s[:, 0:D_STATE] = (coef_re * br - coef_im * bi).astype(BF16)
        wb_s[:, D_STATE:2 * D_STATE] = (coef_re * bi + coef_im * br).astype(BF16)

    lo_q = head_of((tcp, LANES), 1) == 0
    headmean = jnp.where(same_head, 1.0 / HEAD_DIM, 0.0).astype(BF16)
    headmean2 = jnp.concatenate([headmean, headmean], axis=0)
    ones_k = jnp.ones((LANES, 2 * WINDOW) if cache_t else (2 * WINDOW, LANES), BF16)
    ret_lane_head = head_of((tcp, w_ret), 1)

    cos_a, sa_hi, sa_lo = rope_ref[0], rope_ref[1], rope_ref[2]
    cos_r, sr_hi, sr_lo = rope_ref[3], rope_ref[4], rope_ref[5]

    def rope_a(x):
        return (x * cos_a + pltpu.roll(x, ROT_DIM // 2, 1) * sa_hi
                + pltpu.roll(x, LANES - ROT_DIM // 2, 1) * sa_lo)

    def rope_r(x):
        return (x * cos_r + pltpu.roll(x, HEAD_DIM // 2, 1) * sr_hi
                + pltpu.roll(x, LANES - HEAD_DIM // 2, 1) * sr_lo)

    lmin = WINDOW - pos0 - n * tc

    def valid_keys(n_heads):
        qi = lax.broadcasted_iota(jnp.int32, (n_heads * tcp, 2 * WINDOW), 0) & (tcp - 1)
        kl = lax.broadcasted_iota(jnp.int32, (n_heads * tcp, 2 * WINDOW), 1)
        return (kl > qi) & (kl <= qi + WINDOW) & (kl >= lmin)

    valid = valid_keys(1)
    valid2 = valid_keys(2)
    first_head = lax.broadcasted_iota(jnp.int32, (2 * tcp, 1), 0) < tcp
    lo_k = head_of((2 * WINDOW, LANES), 1) == 0

    def norm_body(g, carry):
        b0 = pl.multiple_of(g * gsz, gsz)
        r0 = pl.multiple_of(g * rgrp, rgrp)
        x = h_rows(b0).reshape(rgrp, D_MODEL)
        ms = jnp.mean(x * x, axis=-1, keepdims=True)
        hn = (x * lax.rsqrt(ms + EPS) * ng_ref[...]).astype(BF16)
        hn_s[pl.ds(r0, rgrp), :] = hn
        ub = jnp.dot(hn, win_ref[:, COL_B:COL_B + D_SSM], preferred_element_type=F32)
        u_s[0, pl.ds(r0, rgrp), :] = ub[:, 0:LANES]
        u_s[1, pl.ds(r0, rgrp), :] = ub[:, LANES:2 * LANES]
        gb_s[pl.ds(r0, rgrp), :] = jnp.dot(hn, win_ref[:, COL_B + D_SSM:COL_C],
                                                preferred_element_type=F32)
        return carry

    lax.fori_loop(0, n_groups, norm_body, 0, unroll=True)

    a_re_b = jnp.broadcast_to(a_s[0:1, :], (bblk, D_STATE))
    a_im_b = jnp.broadcast_to(a_s[1:2, :], (bblk, D_STATE))

    def s5_sub_block(si):
        t0 = si * sb
        for t in range(sb):
            for s in range(2):
                ut_s[t * bblk:(t + 1) * bblk, s * LANES:(s + 1) * LANES] = (
                    u_s[s, pl.ds(t0 + t, bblk, stride=tcp), :])
        for c in range(0, 2 * D_STATE, MXU_N):
            bu_s[:, c:c + MXU_N] = _mm(ut_s[...], wb_s[:, c:c + MXU_N])

        def step(t, xs):
            xr, xi = xs
            rt = pl.multiple_of(t * bblk, bblk)
            b_re = bu_s[pl.ds(rt, bblk), 0:D_STATE]
            b_im = bu_s[pl.ds(rt, bblk), D_STATE:2 * D_STATE]
            nxr = a_re_b * xr - a_im_b * xi + b_re
            nxi = a_re_b * xi + a_im_b * xr + b_im
            bu_s[pl.ds(rt, bblk), 0:D_STATE] = nxr
            bu_s[pl.ds(rt, bblk), D_STATE:2 * D_STATE] = nxi
            return nxr, nxi

        xr, xi = lax.fori_loop(0, sb, step, (sre[...], sim[...]), unroll=True)
        sre[...] = xr
        sim[...] = xi
        y = (_mm(bu_s[:, 0:D_STATE], wc_ref[0]) - _mm(bu_s[:, D_STATE:2 * D_STATE], wc_ref[1])
             + d_ref[...] * ut_s[...])
        y = jax.nn.gelu(y)
        y = y * jax.nn.sigmoid(_mm(y, wglu_ref[...]))
        for t in range(sb):
            for s in range(2):
                yb_s[s, pl.ds(t0 + t, bblk, stride=tcp), :] = y[t * bblk:(t + 1) * bblk, s * LANES:(s + 1) * LANES]

    def group_body(g, carry):
        b0 = pl.multiple_of(g * gsz, gsz)
        r0 = pl.multiple_of(g * rgrp, rgrp)
        hn = hn_s[pl.ds(r0, rgrp), :]
        for c in range(0, COL_B - COL_A, MXU_N):
            pa_s[:, c:c + MXU_N] = jnp.dot(hn, win_ref[:, COL_A + c:COL_A + c + MXU_N],
                                           preferred_element_type=F32)
        for c in range(0, D_IN - COL_C, MXU_N):
            pc_s[:, c:c + MXU_N] = jnp.dot(hn, win_ref[:, COL_C + c:COL_C + c + MXU_N],
                                           preferred_element_type=F32)

        def batch_body(j, carry2):
            b = b0 + j
            rj = pl.multiple_of(j * tcp, tcp)
            rb = pl.multiple_of(b * tcp, tcp)
            k_cur = rope_a(pa_s[pl.ds(rj, tcp), D_ATTN:D_ATTN + D_KV])
            v_cur = pa_s[pl.ds(rj, tcp), D_ATTN + D_KV:D_ATTN + 2 * D_KV]
            if cache_t:
                pad = jnp.zeros((WINDOW - tcp, LANES), F32)
                is_new = lax.broadcasted_iota(jnp.int32, (LANES, WINDOW), 1) >= WINDOW - tc
                for buf, cur, out in ((kbuf, k_cur, kout_ref), (vbuf, v_cur, vout_ref)):
                    cur_t = jnp.concatenate([cur, pad], axis=0).T
                    shifted = pltpu.roll(buf[b, :, 0:WINDOW], WINDOW - tc, 1)
                    out[b] = jnp.where(is_new, pltpu.roll(cur_t, WINDOW - tc, 1), shifted)
                    buf[b, :, WINDOW:2 * WINDOW] = cur_t
            else:
                kout_ref[b] = k_cur
                vout_ref[b] = v_cur
                kbuf[b, WINDOW:WINDOW + tcp] = k_cur
                vbuf[b, WINDOW:WINDOW + tcp] = v_cur
            def q_pair(pj):
                return rope_a(pa_s[pl.ds(rj, tcp), pj * LANES:(pj + 1) * LANES])

            def softmax_matmul(q_grp, heads, k_op, v_op):
                q_grp = q_grp * (HEAD_DIM ** -0.5 * LOG2E)
                s_grp = _mm(q_grp, k_op) if cache_t else _mm_nt(q_grp, k_op)
                ex, sink_w = [], []
                for t, hd in enumerate(heads):
                    s = jnp.where(valid, s_grp[t * tcp:(t + 1) * tcp], -jnp.inf)
                    sk = sink_ref[layer0 + lyr, hd] * LOG2E
                    m = jnp.maximum(jnp.max(s, axis=-1, keepdims=True), sk)
                    ex.append(jnp.exp2(s - m))
                    sink_w.append(jnp.exp2(sk - m))
                if cache_t:
                    r = _mm_nt(jnp.concatenate(ex, axis=0), jnp.concatenate([v_op.astype(BF16), ones_k], axis=0))
                else:
                    r = _mm(jnp.concatenate(ex, axis=0), jnp.concatenate([v_op.astype(BF16), ones_k], axis=1))
                o_grp = r[:, 0:LANES] / (r[:, LANES:2 * LANES] + jnp.concatenate(sink_w, axis=0))
                return [o_grp[t * tcp:(t + 1) * tcp] for t in range(len(heads))]

            k_cat, v_cat = kbuf[b], vbuf[b]
            hi_q = jnp.logical_not(lo_q)
            o_heads = [None] * N_Q_HEADS
            if att_group == N_Q_HEADS:
                q_rows = []
                for pj in range(N_Q_HEADS // 2):
                    qp = q_pair(pj)
                    qx = pltpu.roll(qp, HEAD_DIM, 1)
                    kv_lo = (pj // 2 == 0)
                    q_even, q_odd = (qp, qx) if kv_lo else (qx, qp)
                    keep = lo_q if kv_lo else hi_q
                    q_rows += [jnp.where(keep, q_even, 0.0), jnp.where(keep, q_odd, 0.0)]
                o_heads = softmax_matmul(jnp.concatenate(q_rows, axis=0), range(N_Q_HEADS), k_cat, v_cat)
                for hd in range(N_Q_HEADS):
                    if (hd % 2 == 0) != (hd // Q_PER_KV == 0):
                        o_heads[hd] = pltpu.roll(o_heads[hd], HEAD_DIM, 1)
            else:
                k_rol, v_rol = pltpu.roll(k_cat, HEAD_DIM, 1), pltpu.roll(v_cat, HEAD_DIM, 1)
                for kv in range(N_KV_HEADS):
                    qs = jnp.concatenate([q_pair(2 * kv), q_pair(2 * kv + 1)], axis=0)
                    qs = (qs * (HEAD_DIM ** -0.5 * LOG2E)).astype(BF16)
                    o = jnp.zeros((2 * tcp, LANES), F32)
                    for e in range(2):
                        keep = lo_k if e == 0 else jnp.logical_not(lo_k)
                        k_op = jnp.where(keep, k_cat if e == kv else k_rol, 0.0)
                        v_op = jnp.where(keep, v_cat if e == kv else v_rol, 0.0)
                        s = jnp.where(valid2, _mm_nt(qs, k_op), -jnp.inf)
                        sk = jnp.where(first_head, sink_ref[layer0 + lyr, 4 * kv + e],
                                       sink_ref[layer0 + lyr, 4 * kv + 2 + e]) * LOG2E
                        m = jnp.maximum(jnp.max(s, axis=-1, keepdims=True), sk)
                        r = _mm(jnp.exp2(s - m), jnp.concatenate([v_op.astype(BF16), ones_k], axis=1))
                        o = o + r[:, 0:LANES] / (r[:, LANES:2 * LANES] + jnp.exp2(sk - m))
                    o_heads[4 * kv:4 * kv + 4] = [o[0:tcp], o[0:tcp], o[tcp:2 * tcp], o[tcp:2 * tcp]]
            for pj in range(N_Q_HEADS // 2):
                ga = pa_s[pl.ds(rj, tcp),
                          D_ATTN + 2 * D_KV + pj * LANES:D_ATTN + 2 * D_KV + (pj + 1) * LANES]
                ya_s[pl.ds(rb, tcp), pj * LANES:(pj + 1) * LANES] = (
                    jnp.where(lo_q, o_heads[2 * pj], o_heads[2 * pj + 1]) * _silu(ga))
            if n_chunks > 1:
                kbuf[b, 0:WINDOW] = k_cur
                vbuf[b, 0:WINDOW] = v_cur
            for pr in range(n_rg):
                cs = slice(pr * w_ret, (pr + 1) * w_ret)

                def rope_cols(c0):
                    return jnp.concatenate([rope_r(pc_s[pl.ds(rj, tcp), c0 + t * LANES:c0 + (t + 1) * LANES])
                                            for t in range(w_ret // LANES)], axis=1)

                q = rope_cols(pr * w_ret)
                k = rope_cols(D_RET + pr * w_ret) * (HEAD_DIM ** -0.5)
                v = pc_s[pl.ds(rj, tcp), 2 * D_RET + pr * w_ret:2 * D_RET + (pr + 1) * w_ret]
                gr = pc_s[pl.ds(rj, tcp), 3 * D_RET + pr * w_ret:3 * D_RET + (pr + 1) * w_ret]
                r_old = rst[b, pr]
                k_heads = jnp.concatenate([jnp.where(ret_lane_head == hd, k, 0.0) for hd in range(ret_group)], axis=0)
                v_heads = jnp.concatenate([jnp.where(ret_lane_head == hd, v, 0.0) for hd in range(ret_group)], axis=0)
                sc = _mm_nt(q, k_heads) * dmask_ref[pr]
                o = _mm(q, r_old) * rdec_ref[1, :, cs] + _mm(sc, v_heads)
                kd = k * rdec_ref[0, :, cs]
                if tcp < WINDOW:
                    pad = jnp.zeros((WINDOW - tcp, w_ret), F32)
                    d_new = _mm(jnp.concatenate([kd, pad], axis=0).T, jnp.concatenate([v, pad], axis=0))
                else:
                    d_new = _mm(kd.T, v)
                rst[b, pr] = r_old * rdec_ref[2, 0:1, cs] + jnp.where(same_head, d_new, 0.0)
                o2 = o * o
                o2_hi = o2.astype(BF16)
                o2_lo = (o2 - o2_hi.astype(F32)).astype(BF16)
                msq = jnp.dot(jnp.concatenate([o2_hi, o2_lo], axis=1), headmean2, preferred_element_type=F32)
                yc_s[pl.ds(rb, tcp), cs] = o * lax.rsqrt(msq + EPS) * rg_ref[:, cs] * _silu(gr)
            return carry2

        lax.fori_loop(0, gsz, batch_body, 0, unroll=True)
        s5_sub_block(g)
        return carry

    lax.fori_loop(0, n_groups, group_body, 0, unroll=min(n_groups, 4))

    def out_body(g, carry):
        b0 = pl.multiple_of(g * gsz, gsz)
        r0 = pl.multiple_of(g * rgrp, rgrp)
        ya = ya_s[pl.ds(r0, rgrp), :].astype(BF16)
        gate_b = _silu(gb_s[pl.ds(r0, rgrp), :])
        yb0 = (yb_s[0, pl.ds(r0, rgrp), :] * gate_b[:, 0:LANES]).astype(BF16)
        yb1 = (yb_s[1, pl.ds(r0, rgrp), :] * gate_b[:, LANES:2 * LANES]).astype(BF16)
        yc = yc_s[pl.ds(r0, rgrp), :].astype(BF16)
        for c in range(0, D_MODEL, MXU_N):
            cc = slice(c, c + MXU_N)
            y = (jnp.dot(ya, wout_ref[0:D_ATTN, cc], preferred_element_type=F32)
                 + jnp.dot(yb0, wout_ref[D_ATTN:D_ATTN + LANES, cc], preferred_element_type=F32)
                 + jnp.dot(yb1, wout_ref[D_ATTN + LANES:D_ATTN + D_SSM, cc], preferred_element_type=F32)
                 + jnp.dot(yc, wout_ref[D_ATTN + D_SSM:D_MODEL, cc], preferred_element_type=F32))
            h1 = h_rows(b0, cc).reshape(rgrp, MXU_N) + y
            h1_s[:, cc] = h1
            h1b_s[:, cc] = h1.astype(BF16)
        pb = p_ref[pl.ds(b0, gsz)].reshape(rgrp, PLE_DIM).astype(BF16)
        ssq = jnp.zeros((rgrp, 1), F32)
        for c in range(0, D_MODEL, MXU_N):
            cc = slice(c, c + MXU_N)
            gate = jax.nn.sigmoid(jnp.dot(h1b_s[...], wgate_ref[:, cc], preferred_element_type=F32))
            pp = jnp.dot(pb, wpp_ref[:, cc], preferred_element_type=F32)
            h2 = h1_s[:, cc] + gate * pp
            if final_norm:
                ssq = ssq + jnp.sum(h2 * h2, axis=-1, keepdims=True)
                h1_s[:, cc] = h2
            else:
                hout_ref[pl.ds(b0, gsz), :, cc] = h2.reshape(gsz, tcp, MXU_N)

        def write_normed():
            scale = lax.rsqrt(ssq * (1.0 / D_MODEL) + EPS)
            hout_ref[pl.ds(b0, gsz)] = (h1_s[...] * scale * fg_ref[...]).reshape(gsz, tcp, D_MODEL)

        if resident:
            h_all[pl.ds(hb + b0, gsz)] = h1_s[...].reshape(gsz, tcp, D_MODEL)
            pl.when(lyr == n_layers - 1)(write_normed)
        elif final_norm:
            write_normed()
        return carry

    lax.fori_loop(0, n_groups, out_body, 0, unroll=min(n_groups, 2))

    @pl.when(n == n_chunks - 1)
    def _finish():
        sreo_ref[...] = sre[...]
        simo_ref[...] = sim[...]
        r_fin = rst[...]
        ro_ref[...] = sum(r_fin[..., hd * HEAD_DIM:(hd + 1) * HEAD_DIM] for hd in range(ret_group)).reshape(
            bblk, N_RET_HEADS * HEAD_DIM, HEAD_DIM)


def _rope_tables(pos0, t_pad):
    pos = jnp.arange(t_pad, dtype=F32) + pos0
    lane = np.arange(LANES) % HEAD_DIM

    def tables(theta, rot_dim):
        half = rot_dim // 2
        inv = 1.0 / jnp.power(theta, jnp.arange(half, dtype=F32) / half)
        ang = pos[:, None] * inv[None, :]
        cos = jnp.cos(ang)
        sin = jnp.sin(ang)
        idx = np.where(lane < rot_dim, lane % half, 0)
        in_rot = jnp.asarray(lane < rot_dim)
        is_hi = jnp.asarray((lane >= half) & (lane < rot_dim))
        is_lo = jnp.asarray(lane < half)
        cos_l = jnp.where(in_rot[None, :], cos[:, idx], 1.0)
        sin_l = sin[:, idx]
        return (cos_l, jnp.where(is_hi[None, :], sin_l, 0.0), jnp.where(is_lo[None, :], -sin_l, 0.0))

    return jnp.stack(tables(ROPE_THETA, ROT_DIM) + tables(RET_THETA, HEAD_DIM))


def _retention_tables(tc, tcp, ret_group):
    gammas = 1.0 - jnp.power(2.0, -5.0 - jnp.arange(N_RET_HEADS, dtype=F32))
    log_g = jnp.log(gammas)
    i = jnp.arange(tcp, dtype=F32)
    rel = i[:, None] - i[None, :]
    dmask = jnp.where(rel[None] >= 0, jnp.exp(log_g[:, None, None] * jnp.maximum(rel, 0.0)[None]), 0.0)
    kdec = jnp.exp(log_g[None, :] * (tc - 1.0 - i)[:, None])
    qdec = jnp.exp(log_g[None, :] * (i + 1.0)[:, None])
    gch = jnp.broadcast_to(jnp.exp(log_g * tc)[None, :], (tcp, N_RET_HEADS))
    rdec = jnp.stack([jnp.repeat(t, HEAD_DIM, axis=1) for t in (kdec, qdec, gch)])
    dmask = dmask.reshape(N_RET_HEADS // ret_group, ret_group, tcp, tcp).transpose(0, 2, 1, 3).reshape(
        N_RET_HEADS // ret_group, tcp, ret_group * tcp)
    return dmask, rdec


def _layer_call(cfg, batch, h, p_all, rope, dmask, rdec, kwin, vwin, sre0, sim0, r0, weights):
    bblk, tc, tcp, n_chunks, pos0, layer0, n_layers, final_norm, att_group, ret_group = cfg
    w_ret = ret_group * HEAD_DIM
    resident_h = n_layers > 1
    kv_shape = (LANES, 2 * WINDOW) if tcp < WINDOW else (2 * WINDOW, LANES)
    assert not resident_h or (n_chunks == 1 and final_norm)
    rows = bblk * tcp
    rgrp = min(ROW_GROUP, rows)
    rows_sb = min(tc, rgrp // bblk) * bblk
    nb = batch // bblk
    t_pad = n_chunks * tcp
    const = pl.Buffered(1)

    def wspec(shape):
        nd = len(shape)
        return pl.BlockSpec((None,) + tuple(shape[1:]), lambda l, bb, n: (layer0 + l,) + (0,) * (nd - 1),
                            pipeline_mode=const)

    def sspec(shape, lead_layer, mode=None):
        nd = len(shape)
        if lead_layer:
            return pl.BlockSpec((None, bblk) + tuple(shape[2:]),
                                lambda l, bb, n: (layer0 + l, bb) + (0,) * (nd - 2), pipeline_mode=mode)
        return pl.BlockSpec((bblk,) + tuple(shape[1:]), lambda l, bb, n: (bb,) + (0,) * (nd - 1),
                            pipeline_mode=mode)

    big_in_mode = const if resident_h else None

    def ospec(tail):
        return pl.BlockSpec((None, bblk) + tail, lambda l, bb, n: (l, bb) + (0,) * len(tail))

    if resident_h:
        h_in_map = lambda l, bb, n: (jnp.where(l == 0, bb, 0), 0, 0)
        h_out_map = lambda l, bb, n: (jnp.where(l == n_layers - 1, bb, 0), 0, 0)
    else:
        h_in_map = h_out_map = lambda l, bb, n: (bb, n, 0)

    (sink, ng, w_in, lam, bbd, wc, d_row, wglu, rg, w_out, w_pp, w_gate, fg) = weights
    in_specs = [
        pl.BlockSpec((bblk, tcp, D_MODEL), h_in_map, pipeline_mode=big_in_mode),
        pl.BlockSpec((None, bblk, tcp, PLE_DIM), lambda l, bb, n: (layer0 + l, bb, n, 0)),
        pl.BlockSpec((6, tcp, LANES), lambda l, bb, n: (0, n, 0)),
        pl.BlockSpec(dmask.shape, lambda l, bb, n: (0, 0, 0), pipeline_mode=const),
        pl.BlockSpec(rdec.shape, lambda l, bb, n: (0, 0, 0), pipeline_mode=const),
        sspec(kwin.shape, kwin.ndim == 4), sspec(vwin.shape, vwin.ndim == 4),
        sspec(sre0.shape, sre0.ndim == 3), sspec(sim0.shape, sim0.ndim == 3),
        sspec(r0.shape, r0.ndim == 4),
        pl.BlockSpec(memory_space=pltpu.SMEM),
        wspec(ng.shape), wspec(w_in.shape), wspec(lam.shape), wspec(bbd.shape), wspec(wc.shape),
        wspec(d_row.shape), wspec(wglu.shape), wspec(rg.shape), wspec(w_out.shape), wspec(w_pp.shape),
        wspec(w_gate.shape),
        pl.BlockSpec(fg.shape, lambda l, bb, n: (0, 0), pipeline_mode=const),
    ]
    out_shape = (
        jax.ShapeDtypeStruct((batch, t_pad, D_MODEL), F32),
        jax.ShapeDtypeStruct((n_layers, batch, WINDOW, LANES), F32),
        jax.ShapeDtypeStruct((n_layers, batch, WINDOW, LANES), F32),
        jax.ShapeDtypeStruct((n_layers, batch, D_STATE), F32),
        jax.ShapeDtypeStruct((n_layers, batch, D_STATE), F32),
        jax.ShapeDtypeStruct((n_layers, batch, N_RET_HEADS * HEAD_DIM, HEAD_DIM), F32),
    )
    out_specs = (
        pl.BlockSpec((bblk, tcp, D_MODEL), h_out_map),
        ospec((WINDOW, LANES)), ospec((WINDOW, LANES)), ospec((D_STATE,)), ospec((D_STATE,)),
        ospec((N_RET_HEADS * HEAD_DIM, HEAD_DIM)),
    )
    scratch = [
        pltpu.VMEM((bblk,) + kv_shape, F32),
        pltpu.VMEM((bblk,) + kv_shape, F32),
        pltpu.VMEM((bblk, D_STATE), F32),
        pltpu.VMEM((bblk, D_STATE), F32),
        pltpu.VMEM((bblk, N_RET_HEADS // ret_group, w_ret, w_ret), F32),
        pltpu.VMEM((D_SSM, 2 * D_STATE), BF16),
        pltpu.VMEM((2, D_STATE), F32),
        pltpu.VMEM((rows, D_MODEL), BF16),
        pltpu.VMEM((rgrp, COL_B - COL_A), F32),
        pltpu.VMEM((rgrp, D_IN - COL_C), F32),
        pltpu.VMEM((2, rows, LANES), F32),
        pltpu.VMEM((rows, D_SSM), F32),
        pltpu.VMEM((rows, D_ATTN), F32),
        pltpu.VMEM((rows, D_RET), F32),
        pltpu.VMEM((2, rows, LANES), F32),
        pltpu.VMEM((rows_sb, D_SSM), F32),
        pltpu.VMEM((rows_sb, 2 * D_STATE), F32),
        pltpu.VMEM((rgrp, D_MODEL), F32),
        pltpu.VMEM((rgrp, D_MODEL), BF16),
    ]
    if resident_h:
        scratch.append(pltpu.VMEM((batch, tcp, D_MODEL), F32))
    return pl.pallas_call(
        functools.partial(_layer_kernel, cfg),
        grid=(n_layers, nb, n_chunks),
        in_specs=in_specs,
        out_specs=out_specs,
        out_shape=out_shape,
        scratch_shapes=scratch,
        compiler_params=pltpu.CompilerParams(
            dimension_semantics=("arbitrary", "arbitrary", "arbitrary"),
            vmem_limit_bytes=VMEM_LIMIT),
        name="sample_layers" if resident_h else "prompt_layer_%d" % layer0,
    )(h, p_all, rope, dmask, rdec, kwin, vwin, sre0, sim0, r0,
      sink, ng, w_in, lam, bbd, wc, d_row, wglu, rg, w_out, w_pp, w_gate, fg)


def kernel(x_prompt, x_sample, p_prompt, p_sample, cache_win_k, cache_win_v, state_ssm_re, state_ssm_im,
           state_ret, norm_g, w_in, attn_sink, ssm_lam_re, ssm_lam_im, ssm_log_dt, ssm_b_re, ssm_b_im,
           ssm_c_re, ssm_c_im, ssm_d, w_glu, ret_norm_g, w_out, w_ple_proj, w_ple_gate, final_norm_g):
    depth = w_in.shape[0]
    bp, seq = x_prompt.shape[0], x_prompt.shape[1]
    bs, dec_seq = x_sample.shape[0], x_sample.shape[1]
    past_len = 8192

    same_group = jnp.asarray(np.arange(D_SSM)[:, None] // SSM_GROUP == np.arange(D_STATE)[None, :] // SSM_STATE)
    bbd = jnp.stack([
        jnp.where(same_group, jnp.tile(b.transpose(0, 1, 3, 2).reshape(depth, D_SSM, SSM_STATE),
                                       (1, 1, N_SSM_GROUPS)), 0.0)
        for b in (ssm_b_re, ssm_b_im)], axis=1)
    wc = jnp.stack([
        jnp.where(same_group.T, jnp.tile(c.transpose(0, 1, 3, 2).reshape(depth, D_STATE, SSM_GROUP),
                                         (1, 1, N_SSM_GROUPS)), 0.0).astype(BF16)
        for c in (ssm_c_re, ssm_c_im)], axis=1)
    lam = jnp.stack([ssm_lam_re.reshape(depth, D_STATE), ssm_lam_im.reshape(depth, D_STATE),
                     jnp.repeat(ssm_log_dt, SSM_STATE, axis=-1)], axis=1)
    weights = (
        attn_sink,
        norm_g.reshape(depth, 1, D_MODEL), w_in.astype(BF16), lam, bbd, wc,
        ssm_d.reshape(depth, 1, D_SSM), w_glu.astype(BF16), ret_norm_g.reshape(depth, 1, D_RET),
        w_out.astype(BF16), w_ple_proj.astype(BF16), w_ple_gate.astype(BF16),
        final_norm_g.reshape(1, D_MODEL),
    )

    tc_p = WINDOW
    rope_p = _rope_tables(0.0, seq)
    dmask_p, rdec_p = _retention_tables(tc_p, tc_p, GROUPS_PROMPT[1])
    zk = jnp.zeros((bp, WINDOW, LANES), F32)
    zs = jnp.zeros((bp, D_STATE), F32)
    zr = jnp.zeros((bp, N_RET_HEADS * HEAD_DIM, HEAD_DIM), F32)

    tcp_s = 8
    bblk_s = 16
    rope_s = _rope_tables(float(past_len), tcp_s)
    dmask_s, rdec_s = _retention_tables(dec_seq, tcp_s, GROUPS_SAMPLE[1])
    pad_t = ((0, 0), (0, tcp_s - dec_seq), (0, 0))
    hs = jnp.pad(x_sample, pad_t)
    ps = jnp.pad(p_sample, ((0, 0),) + pad_t)
    kwin_s = cache_win_k.transpose(0, 1, 3, 4, 2).reshape(depth, bs, LANES, WINDOW)
    vwin_s = cache_win_v.transpose(0, 1, 3, 4, 2).reshape(depth, bs, LANES, WINDOW)
    sre_s = state_ssm_re.reshape(depth, bs, D_STATE)
    sim_s = state_ssm_im.reshape(depth, bs, D_STATE)
    r_s = state_ret.reshape(depth, bs, N_RET_HEADS * HEAD_DIM, HEAD_DIM)

    hp = x_prompt
    outs_p = []
    for i in range(depth):
        cfg_p = (bp, tc_p, tc_p, seq // tc_p, 0, i, 1, i == depth - 1) + GROUPS_PROMPT
        hp, *st = _layer_call(cfg_p, bp, hp, p_prompt, rope_p, dmask_p, rdec_p, zk, zk, zs, zs, zr, weights)
        outs_p.append(st)
    st_p = [jnp.concatenate(t, axis=0) for t in zip(*outs_p)]
    cfg_s = (bblk_s, dec_seq, tcp_s, 1, past_len, 0, depth, True) + GROUPS_SAMPLE
    hs, *st_s = _layer_call(cfg_s, bs, hs, ps, rope_s, dmask_s, rdec_s, kwin_s, vwin_s, sre_s, sim_s, r_s, weights)

    def pack(st, b, transposed):
        k, v, sr, si, r = st
        if transposed:
            k, v = (t.reshape(depth, b, N_KV_HEADS, HEAD_DIM, WINDOW).transpose(0, 1, 4, 2, 3) for t in (k, v))
        return (k.reshape(depth, b, WINDOW, N_KV_HEADS, HEAD_DIM), v.reshape(depth, b, WINDOW, N_KV_HEADS, HEAD_DIM),
                sr.reshape(depth, b, N_SSM_GROUPS, SSM_STATE), si.reshape(depth, b, N_SSM_GROUPS, SSM_STATE),
                r.reshape(depth, b, N_RET_HEADS, HEAD_DIM, HEAD_DIM))

    return (hp, hs[:, :dec_seq]) + pack(st_p, bp, False) + pack(st_s, bs, True)
```

```python
import functools
import math

import numpy as np
import jax
import jax.numpy as jnp
from jax import lax
from jax.experimental import pallas as pl
from jax.experimental.pallas import tpu as pltpu

F32 = jnp.float32
BF16 = jnp.bfloat16

D_MODEL = 1024
HEAD_DIM = 64
D_ATTN = 512
N_Q_HEADS = 8
N_KV_HEADS = 2
Q_PER_KV = N_Q_HEADS // N_KV_HEADS
D_KV = 128
WINDOW = 128
ROPE_THETA = 500000.0
ROT_DIM = 16
D_SSM = 256
SSM_GROUP = 16
N_SSM_GROUPS = 16
SSM_STATE = 64
D_STATE = N_SSM_GROUPS * SSM_STATE
D_RET = 256
N_RET_HEADS = 4
RET_THETA = 10000.0
PLE_DIM = 256
EPS = 1e-6
LOG2E = math.log2(math.e)
D_IN = 2816
COL_A = 0
COL_B = 1280
COL_C = 1792
LANES = 128
MXU_N = 256
ROW_GROUP = 256
VMEM_LIMIT = 58 * 1024 * 1024
GROUPS_PROMPT = (2, 2)
GROUPS_SAMPLE = (8, 2)


def _mm(a, b):
    return jnp.dot(a.astype(BF16), b.astype(BF16), preferred_element_type=F32)


def _mm_nt(a, b):
    return lax.dot_general(a.astype(BF16), b.astype(BF16), (((1,), (1,)), ((), ())),
                           preferred_element_type=F32)


def _silu(x):
    return x * jax.nn.sigmoid(x)


def _layer_kernel(cfg,
                  h_ref, p_ref, rope_ref, dmask_ref, rdec_ref,
                  kwin_ref, vwin_ref, sre0_ref, sim0_ref, r0_ref,
                  sink_ref, ng_ref, win_ref, lam_ref, bbd_ref, wc_ref, d_ref, wglu_ref, rg_ref,
                  wout_ref, wpp_ref, wgate_ref, fg_ref,
                  hout_ref, kout_ref, vout_ref, sreo_ref, simo_ref, ro_ref,
                  kbuf, vbuf, sre, sim, rst, wb_s, a_s, hn_s, pa_s, pc_s, u_s, gb_s,
                  ya_s, yc_s, yb_s, ut_s, bu_s, h1_s, h1b_s, *resident):
    bblk, tc, tcp, n_chunks, pos0, layer0, n_layers, final_norm, att_group, ret_group = cfg
    w_ret = ret_group * HEAD_DIM
    n_rg = N_RET_HEADS // ret_group
    rgrp = min(ROW_GROUP, bblk * tcp)
    gsz = rgrp // tcp
    n_groups = bblk // gsz
    sb = min(tc, rgrp // bblk)
    assert tc // sb == n_groups
    assert tcp & (tcp - 1) == 0
    cache_t = tcp < WINDOW
    assert not cache_t or (n_chunks == 1 and att_group == N_Q_HEADS)
    lyr = pl.program_id(0)
    n = pl.program_id(2)
    if resident:
        h_all, = resident
        hb = pl.multiple_of(pl.program_id(1) * bblk, bblk)

        @pl.when(lyr == 0)
        def _load_h():
            h_all[pl.ds(hb, bblk)] = h_ref[...]

        def h_rows(b0, cols=slice(None)):
            return h_all[pl.ds(hb + b0, gsz), :, cols]
    else:
        def h_rows(b0, cols=slice(None)):
            return h_ref[pl.ds(b0, gsz), :, cols]

    def head_of(shape, dim):
        return lax.broadcasted_iota(jnp.int32, shape, dim) >> (HEAD_DIM.bit_length() - 1)

    same_head = head_of((w_ret, w_ret), 0) == head_of((w_ret, w_ret), 1)

    @pl.when(n == 0)
    def _init():
        if cache_t:
            kbuf[:, :, 0:WINDOW] = kwin_ref[...]
            vbuf[:, :, 0:WINDOW] = vwin_ref[...]
        else:
            kbuf[:, 0:WINDOW] = kwin_ref[...]
            vbuf[:, 0:WINDOW] = vwin_ref[...]
        sre[...] = sre0_ref[...]
        sim[...] = sim0_ref[...]
        r_in = r0_ref[...].reshape(bblk, n_rg, w_ret, HEAD_DIM)
        rst[...] = jnp.where(same_head, jnp.concatenate([r_in] * ret_group, axis=-1), 0.0)
        if tc < tcp:
            yb_s[...] = jnp.zeros_like(yb_s)

    @pl.when((n == 0) & (pl.program_id(1) == 0))
    def _discretise():
        lr = lam_ref[0:1, :]
        li = lam_ref[1:2, :]
        dt = jnp.exp(lam_ref[2:3, :])
        mag = jnp.exp(lr * dt)
        a_re = mag * jnp.cos(li * dt)
        a_im = mag * jnp.sin(li * dt)
        nr = a_re - 1.0
        ni = a_im
        den = lr * lr + li * li
        coef_re = (nr * lr + ni * li) / den
        coef_im = (ni * lr - nr * li) / den
        a_s[0:1, :] = a_re
        a_s[1:2, :] = a_im
        br = bbd_ref[0]
        bi = bbd_ref[1]
        wb_s[:, 0:D_STATE] = (coef_re * br - coef_im * bi).astype(BF16)
        wb_s[:, D_STATE:2 * D_STATE] = (coef_re * bi + coef_im * br).astype(BF16)

    lo_q = head_of((tcp, LANES), 1) == 0
    headmean = jnp.where(same_head, 1.0 / HEAD_DIM, 0.0).astype(BF16)
    headmean2 = jnp.concatenate([headmean, headmean], axis=0)
    ones_k = jnp.ones((LANES, 2 * WINDOW) if cache_t else (2 * WINDOW, LANES), BF16)
    ret_lane_head = head_of((tcp, w_ret), 1)

    cos_a, sa_hi, sa_lo = rope_ref[0], rope_ref[1], rope_ref[2]
    cos_r, sr_hi, sr_lo = rope_ref[3], rope_ref[4], rope_ref[5]

    def rope_a(x):
        return (x * cos_a + pltpu.roll(x, ROT_DIM // 2, 1) * sa_hi
                + pltpu.roll(x, LANES - ROT_DIM // 2, 1) * sa_lo)

    def rope_r(x):
        return (x * cos_r + pltpu.roll(x, HEAD_DIM // 2, 1) * sr_hi
                + pltpu.roll(x, LANES - HEAD_DIM // 2, 1) * sr_lo)

    lmin = WINDOW - pos0 - n * tc

    def valid_keys(n_heads):
        qi = lax.broadcasted_iota(jnp.int32, (n_heads * tcp, 2 * WINDOW), 0) & (tcp - 1)
        kl = lax.broadcasted_iota(jnp.int32, (n_heads * tcp, 2 * WINDOW), 1)
        return (kl > qi) & (kl <= qi + WINDOW) & (kl >= lmin)

    valid = valid_keys(1)
    valid2 = valid_keys(2)
    first_head = lax.broadcasted_iota(jnp.int32, (2 * tcp, 1), 0) < tcp
    lo_k = head_of((2 * WINDOW, LANES), 1) == 0

    def norm_body(g, carry):
        b0 = pl.multiple_of(g * gsz, gsz)
        r0 = pl.multiple_of(g * rgrp, rgrp)
        x = h_rows(b0).reshape(rgrp, D_MODEL)
        ms = jnp.mean(x * x, axis=-1, keepdims=True)
        hn = (x * lax.rsqrt(ms + EPS) * ng_ref[...]).astype(BF16)
        hn_s[pl.ds(r0, rgrp), :] = hn
        ub = jnp.dot(hn, win_ref[:, COL_B:COL_B + D_SSM], preferred_element_type=F32)
        u_s[0, pl.ds(r0, rgrp), :] = ub[:, 0:LANES]
        u_s[1, pl.ds(r0, rgrp), :] = ub[:, LANES:2 * LANES]
        gb_s[pl.ds(r0, rgrp), :] = jnp.dot(hn, win_ref[:, COL_B + D_SSM:COL_C],
                                                preferred_element_type=F32)
        return carry

    lax.fori_loop(0, n_groups, norm_body, 0, unroll=True)

    a_re_b = jnp.broadcast_to(a_s[0:1, :], (bblk, D_STATE))
    a_im_b = jnp.broadcast_to(a_s[1:2, :], (bblk, D_STATE))

    def s5_sub_block(si):
        t0 = si * sb
        for t in range(sb):
            for s in range(2):
                ut_s[t * bblk:(t + 1) * bblk, s * LANES:(s + 1) * LANES] = (
                    u_s[s, pl.ds(t0 + t, bblk, stride=tcp), :])
        for c in range(0, 2 * D_STATE, MXU_N):
            bu_s[:, c:c + MXU_N] = _mm(ut_s[...], wb_s[:, c:c + MXU_N])

        def step(t, xs):
            xr, xi = xs
            rt = pl.multiple_of(t * bblk, bblk)
            b_re = bu_s[pl.ds(rt, bblk), 0:D_STATE]
            b_im = bu_s[pl.ds(rt, bblk), D_STATE:2 * D_STATE]
            nxr = a_re_b * xr - a_im_b * xi + b_re
            nxi = a_re_b * xi + a_im_b * xr + b_im
            bu_s[pl.ds(rt, bblk), 0:D_STATE] = nxr
            bu_s[pl.ds(rt, bblk), D_STATE:2 * D_STATE] = nxi
            return nxr, nxi

        xr, xi = lax.fori_loop(0, sb, step, (sre[...], sim[...]), unroll=True)
        sre[...] = xr
        sim[...] = xi
        y = (_mm(bu_s[:, 0:D_STATE], wc_ref[0]) - _mm(bu_s[:, D_STATE:2 * D_STATE], wc_ref[1])
             + d_ref[...] * ut_s[...])
        y = jax.nn.gelu(y)
        y = y * jax.nn.sigmoid(_mm(y, wglu_ref[...]))
        for t in range(sb):
            for s in range(2):
                yb_s[s, pl.ds(t0 + t, bblk, stride=tcp), :] = y[t * bblk:(t + 1) * bblk, s * LANES:(s + 1) * LANES]

    def group_body(g, carry):
        b0 = pl.multiple_of(g * gsz, gsz)
        r0 = pl.multiple_of(g * rgrp, rgrp)
        hn = hn_s[pl.ds(r0, rgrp), :]
        for c in range(0, COL_B - COL_A, MXU_N):
            pa_s[:, c:c + MXU_N] = jnp.dot(hn, win_ref[:, COL_A + c:COL_A + c + MXU_N],
                                           preferred_element_type=F32)
        for c in range(0, D_IN - COL_C, MXU_N):
            pc_s[:, c:c + MXU_N] = jnp.dot(hn, win_ref[:, COL_C + c:COL_C + c + MXU_N],
                                           preferred_element_type=F32)

        def batch_body(j, carry2):
            b = b0 + j
            rj = pl.multiple_of(j * tcp, tcp)
            rb = pl.multiple_of(b * tcp, tcp)
            k_cur = rope_a(pa_s[pl.ds(rj, tcp), D_ATTN:D_ATTN + D_KV])
            v_cur = pa_s[pl.ds(rj, tcp), D_ATTN + D_KV:D_ATTN + 2 * D_KV]
            if cache_t:
                pad = jnp.zeros((WINDOW - tcp, LANES), F32)
                is_new = lax.broadcasted_iota(jnp.int32, (LANES, WINDOW), 1) >= WINDOW - tc
                for buf, cur, out in ((kbuf, k_cur, kout_ref), (vbuf, v_cur, vout_ref)):
                    cur_t = jnp.concatenate([cur, pad], axis=0).T
                    shifted = pltpu.roll(buf[b, :, 0:WINDOW], WINDOW - tc, 1)
                    out[b] = jnp.where(is_new, pltpu.roll(cur_t, WINDOW - tc, 1), shifted)
                    buf[b, :, WINDOW:2 * WINDOW] = cur_t
            else:
                kout_ref[b] = k_cur
                vout_ref[b] = v_cur
                kbuf[b, WINDOW:WINDOW + tcp] = k_cur
                vbuf[b, WINDOW:WINDOW + tcp] = v_cur
            def q_pair(pj):
                return rope_a(pa_s[pl.ds(rj, tcp), pj * LANES:(pj + 1) * LANES])

            def softmax_matmul(q_grp, heads, k_op, v_op):
                q_grp = q_grp * (HEAD_DIM ** -0.5 * LOG2E)
                s_grp = _mm(q_grp, k_op) if cache_t else _mm_nt(q_grp, k_op)
                ex, sink_w = [], []
                for t, hd in enumerate(heads):
                    s = jnp.where(valid, s_grp[t * tcp:(t + 1) * tcp], -jnp.inf)
                    sk = sink_ref[layer0 + lyr, hd] * LOG2E
                    m = jnp.maximum(jnp.max(s, axis=-1, keepdims=True), sk)
                    ex.append(jnp.exp2(s - m))
                    sink_w.append(jnp.exp2(sk - m))
                if cache_t:
                    r = _mm_nt(jnp.concatenate(ex, axis=0), jnp.concatenate([v_op.astype(BF16), ones_k], axis=0))
                else:
                    r = _mm(jnp.concatenate(ex, axis=0), jnp.concatenate([v_op.astype(BF16), ones_k], axis=1))
                o_grp = r[:, 0:LANES] / (r[:, LANES:2 * LANES] + jnp.concatenate(sink_w, axis=0))
                return [o_grp[t * tcp:(t + 1) * tcp] for t in range(len(heads))]

            k_cat, v_cat = kbuf[b], vbuf[b]
            hi_q = jnp.logical_not(lo_q)
            o_heads = [None] * N_Q_HEADS
            if att_group == N_Q_HEADS:
                q_rows = []
                for pj in range(N_Q_HEADS // 2):
                    qp = q_pair(pj)
                    qx = pltpu.roll(qp, HEAD_DIM, 1)
                    kv_lo = (pj // 2 == 0)
                    q_even, q_odd = (qp, qx) if kv_lo else (qx, qp)
                    keep = lo_q if kv_lo else hi_q
                    q_rows += [jnp.where(keep, q_even, 0.0), jnp.where(keep, q_odd, 0.0)]
                o_heads = softmax_matmul(jnp.concatenate(q_rows, axis=0), range(N_Q_HEADS), k_cat, v_cat)
                for hd in range(N_Q_HEADS):
                    if (hd % 2 == 0) != (hd // Q_PER_KV == 0):
                        o_heads[hd] = pltpu.roll(o_heads[hd], HEAD_DIM, 1)
            else:
                k_rol, v_rol = pltpu.roll(k_cat, HEAD_DIM, 1), pltpu.roll(v_cat, HEAD_DIM, 1)
                for kv in range(N_KV_HEADS):
                    qs = jnp.concatenate([q_pair(2 * kv), q_pair(2 * kv + 1)], axis=0)
                    qs = (qs * (HEAD_DIM ** -0.5 * LOG2E)).astype(BF16)
                    o = jnp.zeros((2 * tcp, LANES), F32)
                    for e in range(2):
                        keep = lo_k if e == 0 else jnp.logical_not(lo_k)
                        k_op = jnp.where(keep, k_cat if e == kv else k_rol, 0.0)
                        v_op = jnp.where(keep, v_cat if e == kv else v_rol, 0.0)
                        s = jnp.where(valid2, _mm_nt(qs, k_op), -jnp.inf)
                        sk = jnp.where(first_head, sink_ref[layer0 + lyr, 4 * kv + e],
                                       sink_ref[layer0 + lyr, 4 * kv + 2 + e]) * LOG2E
                        m = jnp.maximum(jnp.max(s, axis=-1, keepdims=True), sk)
                        r = _mm(jnp.exp2(s - m), jnp.concatenate([v_op.astype(BF16), ones_k], axis=1))
                        o = o + r[:, 0:LANES] / (r[:, LANES:2 * LANES] + jnp.exp2(sk - m))
                    o_heads[4 * kv:4 * kv + 4] = [o[0:tcp], o[0:tcp], o[tcp:2 * tcp], o[tcp:2 * tcp]]
            for pj in range(N_Q_HEADS // 2):
                ga = pa_s[pl.ds(rj, tcp),
                          D_ATTN + 2 * D_KV + pj * LANES:D_ATTN + 2 * D_KV + (pj + 1) * LANES]
                ya_s[pl.ds(rb, tcp), pj * LANES:(pj + 1) * LANES] = (
                    jnp.where(lo_q, o_heads[2 * pj], o_heads[2 * pj + 1]) * _silu(ga))
            if n_chunks > 1:
                kbuf[b, 0:WINDOW] = k_cur
                vbuf[b, 0:WINDOW] = v_cur
            for pr in range(n_rg):
                cs = slice(pr * w_ret, (pr + 1) * w_ret)

                def rope_cols(c0):
                    return jnp.concatenate([rope_r(pc_s[pl.ds(rj, tcp), c0 + t * LANES:c0 + (t + 1) * LANES])
                                            for t in range(w_ret // LANES)], axis=1)

                q = rope_cols(pr * w_ret)
                k = rope_cols(D_RET + pr * w_ret) * (HEAD_DIM ** -0.5)
                v = pc_s[pl.ds(rj, tcp), 2 * D_RET + pr * w_ret:2 * D_RET + (pr + 1) * w_ret]
                gr = pc_s[pl.ds(rj, tcp), 3 * D_RET + pr * w_ret:3 * D_RET + (pr + 1) * w_ret]
                r_old = rst[b, pr]
                k_heads = jnp.concatenate([jnp.where(ret_lane_head == hd, k, 0.0) for hd in range(ret_group)], axis=0)
                v_heads = jnp.concatenate([jnp.where(ret_lane_head == hd, v, 0.0) for hd in range(ret_group)], axis=0)
                sc = _mm_nt(q, k_heads) * dmask_ref[pr]
                o = _mm(q, r_old) * rdec_ref[1, :, cs] + _mm(sc, v_heads)
                kd = k * rdec_ref[0, :, cs]
                if tcp < WINDOW:
                    pad = jnp.zeros((WINDOW - tcp, w_ret), F32)
                    d_new = _mm(jnp.concatenate([kd, pad], axis=0).T, jnp.concatenate([v, pad], axis=0))
                else:
                    d_new = _mm(kd.T, v)
                rst[b, pr] = r_old * rdec_ref[2, 0:1, cs] + jnp.where(same_head, d_new, 0.0)
                o2 = o * o
                o2_hi = o2.astype(BF16)
                o2_lo = (o2 - o2_hi.astype(F32)).astype(BF16)
                msq = jnp.dot(jnp.concatenate([o2_hi, o2_lo], axis=1), headmean2, preferred_element_type=F32)
                yc_s[pl.ds(rb, tcp), cs] = o * lax.rsqrt(msq + EPS) * rg_ref[:, cs] * _silu(gr)
            return carry2

        lax.fori_loop(0, gsz, batch_body, 0, unroll=True)
        s5_sub_block(g)
        return carry

    lax.fori_loop(0, n_groups, group_body, 0, unroll=min(n_groups, 4))

    def out_body(g, carry):
        b0 = pl.multiple_of(g * gsz, gsz)
        r0 = pl.multiple_of(g * rgrp, rgrp)
        ya = ya_s[pl.ds(r0, rgrp), :].astype(BF16)
        gate_b = _silu(gb_s[pl.ds(r0, rgrp), :])
        yb0 = (yb_s[0, pl.ds(r0, rgrp), :] * gate_b[:, 0:LANES]).astype(BF16)
        yb1 = (yb_s[1, pl.ds(r0, rgrp), :] * gate_b[:, LANES:2 * LANES]).astype(BF16)
        yc = yc_s[pl.ds(r0, rgrp), :].astype(BF16)
        for c in range(0, D_MODEL, MXU_N):
            cc = slice(c, c + MXU_N)
            y = (jnp.dot(ya, wout_ref[0:D_ATTN, cc], preferred_element_type=F32)
                 + jnp.dot(yb0, wout_ref[D_ATTN:D_ATTN + LANES, cc], preferred_element_type=F32)
                 + jnp.dot(yb1, wout_ref[D_ATTN + LANES:D_ATTN + D_SSM, cc], preferred_element_type=F32)
                 + jnp.dot(yc, wout_ref[D_ATTN + D_SSM:D_MODEL, cc], preferred_element_type=F32))
            h1 = h_rows(b0, cc).reshape(rgrp, MXU_N) + y
            h1_s[:, cc] = h1
            h1b_s[:, cc] = h1.astype(BF16)
        pb = p_ref[pl.ds(b0, gsz)].reshape(rgrp, PLE_DIM).astype(BF16)
        ssq = jnp.zeros((rgrp, 1), F32)
        for c in range(0, D_MODEL, MXU_N):
            cc = slice(c, c + MXU_N)
            gate = jax.nn.sigmoid(jnp.dot(h1b_s[...], wgate_ref[:, cc], preferred_element_type=F32))
            pp = jnp.dot(pb, wpp_ref[:, cc], preferred_element_type=F32)
            h2 = h1_s[:, cc] + gate * pp
            if final_norm:
                ssq = ssq + jnp.sum(h2 * h2, axis=-1, keepdims=True)
                h1_s[:, cc] = h2
            else:
                hout_ref[pl.ds(b0, gsz), :, cc] = h2.reshape(gsz, tcp, MXU_N)

        def write_normed():
            scale = lax.rsqrt(ssq * (1.0 / D_MODEL) + EPS)
            normed = (h1_s[...] * scale * fg_ref[...]).reshape(gsz, tcp, D_MODEL)
            hout_ref[pl.ds(b0, gsz)] = normed if hout_ref.shape[1] == tcp else normed[:, 0:tc, :]

        if resident:
            h_all[pl.ds(hb + b0, gsz)] = h1_s[...].reshape(gsz, tcp, D_MODEL)
            pl.when(lyr == n_layers - 1)(write_normed)
        elif final_norm:
            write_normed()
        return carry

    lax.fori_loop(0, n_groups, out_body, 0, unroll=True)

    @pl.when(n == n_chunks - 1)
    def _finish():
        sreo_ref[...] = sre[...]
        simo_ref[...] = sim[...]
        r_fin = rst[...]
        ro_ref[...] = sum(r_fin[..., hd * HEAD_DIM:(hd + 1) * HEAD_DIM] for hd in range(ret_group)).reshape(
            bblk, N_RET_HEADS * HEAD_DIM, HEAD_DIM)


def _rope_tables(pos0, t_pad):
    pos = jnp.arange(t_pad, dtype=F32) + pos0
    lane = np.arange(LANES) % HEAD_DIM

    def tables(theta, rot_dim):
        half = rot_dim // 2
        inv = 1.0 / jnp.power(theta, jnp.arange(half, dtype=F32) / half)
        ang = pos[:, None] * inv[None, :]
        cos = jnp.cos(ang)
        sin = jnp.sin(ang)
        idx = np.where(lane < rot_dim, lane % half, 0)
        in_rot = jnp.asarray(lane < rot_dim)
        is_hi = jnp.asarray((lane >= half) & (lane < rot_dim))
        is_lo = jnp.asarray(lane < half)
        cos_l = jnp.where(in_rot[None, :], cos[:, idx], 1.0)
        sin_l = sin[:, idx]
        return (cos_l, jnp.where(is_hi[None, :], sin_l, 0.0), jnp.where(is_lo[None, :], -sin_l, 0.0))

    return jnp.stack(tables(ROPE_THETA, ROT_DIM) + tables(RET_THETA, HEAD_DIM))


def _retention_tables(tc, tcp, ret_group):
    gammas = 1.0 - jnp.power(2.0, -5.0 - jnp.arange(N_RET_HEADS, dtype=F32))
    log_g = jnp.log(gammas)
    i = jnp.arange(tcp, dtype=F32)
    rel = i[:, None] - i[None, :]
    dmask = jnp.where(rel[None] >= 0, jnp.exp(log_g[:, None, None] * jnp.maximum(rel, 0.0)[None]), 0.0)
    kdec = jnp.exp(log_g[None, :] * (tc - 1.0 - i)[:, None])
    qdec = jnp.exp(log_g[None, :] * (i + 1.0)[:, None])
    gch = jnp.broadcast_to(jnp.exp(log_g * tc)[None, :], (tcp, N_RET_HEADS))
    rdec = jnp.stack([jnp.repeat(t, HEAD_DIM, axis=1) for t in (kdec, qdec, gch)])
    dmask = dmask.reshape(N_RET_HEADS // ret_group, ret_group, tcp, tcp).transpose(0, 2, 1, 3).reshape(
        N_RET_HEADS // ret_group, tcp, ret_group * tcp)
    return dmask, rdec


def _layer_call(cfg, batch, h, p_all, rope, dmask, rdec, kwin, vwin, sre0, sim0, r0, weights):
    bblk, tc, tcp, n_chunks, pos0, layer0, n_layers, final_norm, att_group, ret_group = cfg
    w_ret = ret_group * HEAD_DIM
    resident_h = n_layers > 1
    kv_shape = (LANES, 2 * WINDOW) if tcp < WINDOW else (2 * WINDOW, LANES)
    assert not resident_h or (n_chunks == 1 and final_norm)
    rows = bblk * tcp
    rgrp = min(ROW_GROUP, rows)
    rows_sb = min(tc, rgrp // bblk) * bblk
    nb = batch // bblk
    t_pad = n_chunks * tcp
    const = pl.Buffered(1)

    def wspec(shape):
        nd = len(shape)
        return pl.BlockSpec((None,) + tuple(shape[1:]), lambda l, bb, n: (layer0 + l,) + (0,) * (nd - 1),
                            pipeline_mode=const)

    def sspec(shape, lead_layer, mode=None):
        nd = len(shape)
        if lead_layer:
            return pl.BlockSpec((None, bblk) + tuple(shape[2:]),
                                lambda l, bb, n: (layer0 + l, bb) + (0,) * (nd - 2), pipeline_mode=mode)
        return pl.BlockSpec((bblk,) + tuple(shape[1:]), lambda l, bb, n: (bb,) + (0,) * (nd - 1),
                            pipeline_mode=mode)

    big_in_mode = const if resident_h else None

    def ospec(tail):
        return pl.BlockSpec((None, bblk) + tail, lambda l, bb, n: (l, bb) + (0,) * len(tail))

    if resident_h:
        h_in_map = lambda l, bb, n: (jnp.where(l == 0, bb, 0), 0, 0)
        h_out_map = lambda l, bb, n: (jnp.where(l == n_layers - 1, bb, 0), 0, 0)
    else:
        h_in_map = h_out_map = lambda l, bb, n: (bb, n, 0)

    (sink, ng, w_in, lam, bbd, wc, d_row, wglu, rg, w_out, w_pp, w_gate, fg) = weights
    in_specs = [
        pl.BlockSpec((bblk, tcp, D_MODEL), h_in_map, pipeline_mode=big_in_mode),
        pl.BlockSpec((None, bblk, tcp, PLE_DIM), lambda l, bb, n: (layer0 + l, bb, n, 0)),
        pl.BlockSpec((6, tcp, LANES), lambda l, bb, n: (0, n, 0)),
        pl.BlockSpec(dmask.shape, lambda l, bb, n: (0, 0, 0), pipeline_mode=const),
        pl.BlockSpec(rdec.shape, lambda l, bb, n: (0, 0, 0), pipeline_mode=const),
        sspec(kwin.shape, kwin.ndim == 4), sspec(vwin.shape, vwin.ndim == 4),
        sspec(sre0.shape, sre0.ndim == 3), sspec(sim0.shape, sim0.ndim == 3),
        sspec(r0.shape, r0.ndim == 4),
        pl.BlockSpec(memory_space=pltpu.SMEM),
        wspec(ng.shape), wspec(w_in.shape), wspec(lam.shape), wspec(bbd.shape), wspec(wc.shape),
        wspec(d_row.shape), wspec(wglu.shape), wspec(rg.shape), wspec(w_out.shape), wspec(w_pp.shape),
        wspec(w_gate.shape),
        pl.BlockSpec(fg.shape, lambda l, bb, n: (0, 0), pipeline_mode=const),
    ]
    out_shape = (
        jax.ShapeDtypeStruct((batch, tc if resident_h else t_pad, D_MODEL), F32),
        jax.ShapeDtypeStruct((n_layers, batch, WINDOW, LANES), F32),
        jax.ShapeDtypeStruct((n_layers, batch, WINDOW, LANES), F32),
        jax.ShapeDtypeStruct((n_layers, batch, D_STATE), F32),
        jax.ShapeDtypeStruct((n_layers, batch, D_STATE), F32),
        jax.ShapeDtypeStruct((n_layers, batch, N_RET_HEADS * HEAD_DIM, HEAD_DIM), F32),
    )
    out_specs = (
        pl.BlockSpec((bblk, tc if resident_h else tcp, D_MODEL), h_out_map),
        ospec((WINDOW, LANES)), ospec((WINDOW, LANES)), ospec((D_STATE,)), ospec((D_STATE,)),
        ospec((N_RET_HEADS * HEAD_DIM, HEAD_DIM)),
    )
    scratch = [
        pltpu.VMEM((bblk,) + kv_shape, F32),
        pltpu.VMEM((bblk,) + kv_shape, F32),
        pltpu.VMEM((bblk, D_STATE), F32),
        pltpu.VMEM((bblk, D_STATE), F32),
        pltpu.VMEM((bblk, N_RET_HEADS // ret_group, w_ret, w_ret), F32),
        pltpu.VMEM((D_SSM, 2 * D_STATE), BF16),
        pltpu.VMEM((2, D_STATE), F32),
        pltpu.VMEM((rows, D_MODEL), BF16),
        pltpu.VMEM((rgrp, COL_B - COL_A), F32),
        pltpu.VMEM((rgrp, D_IN - COL_C), F32),
        pltpu.VMEM((2, rows, LANES), F32),
        pltpu.VMEM((rows, D_SSM), F32),
        pltpu.VMEM((rows, D_ATTN), F32),
        pltpu.VMEM((rows, D_RET), F32),
        pltpu.VMEM((2, rows, LANES), F32),
        pltpu.VMEM((rows_sb, D_SSM), F32),
        pltpu.VMEM((rows_sb, 2 * D_STATE), F32),
        pltpu.VMEM((rgrp, D_MODEL), F32),
        pltpu.VMEM((rgrp, D_MODEL), BF16),
    ]
    if resident_h:
        scratch.append(pltpu.VMEM((batch, tcp, D_MODEL), F32))
    return pl.pallas_call(
        functools.partial(_layer_kernel, cfg),
        grid=(n_layers, nb, n_chunks),
        in_specs=in_specs,
        out_specs=out_specs,
        out_shape=out_shape,
        scratch_shapes=scratch,
        compiler_params=pltpu.CompilerParams(
            dimension_semantics=("arbitrary", "arbitrary", "arbitrary"),
            vmem_limit_bytes=VMEM_LIMIT),
        name="sample_layers" if resident_h else "prompt_layer_%d" % layer0,
    )(h, p_all, rope, dmask, rdec, kwin, vwin, sre0, sim0, r0,
      sink, ng, w_in, lam, bbd, wc, d_row, wglu, rg, w_out, w_pp, w_gate, fg)


def kernel(x_prompt, x_sample, p_prompt, p_sample, cache_win_k, cache_win_v, state_ssm_re, state_ssm_im,
           state_ret, norm_g, w_in, attn_sink, ssm_lam_re, ssm_lam_im, ssm_log_dt, ssm_b_re, ssm_b_im,
           ssm_c_re, ssm_c_im, ssm_d, w_glu, ret_norm_g, w_out, w_ple_proj, w_ple_gate, final_norm_g):
    depth = w_in.shape[0]
    bp, seq = x_prompt.shape[0], x_prompt.shape[1]
    bs, dec_seq = x_sample.shape[0], x_sample.shape[1]
    past_len = 8192

    same_group = jnp.asarray(np.arange(D_SSM)[:, None] // SSM_GROUP == np.arange(D_STATE)[None, :] // SSM_STATE)
    bbd = jnp.stack([
        jnp.where(same_group, jnp.tile(b.transpose(0, 1, 3, 2).reshape(depth, D_SSM, SSM_STATE),
                                       (1, 1, N_SSM_GROUPS)), 0.0)
        for b in (ssm_b_re, ssm_b_im)], axis=1)
    wc = jnp.stack([
        jnp.where(same_group.T, jnp.tile(c.transpose(0, 1, 3, 2).reshape(depth, D_STATE, SSM_GROUP),
                                         (1, 1, N_SSM_GROUPS)), 0.0).astype(BF16)
        for c in (ssm_c_re, ssm_c_im)], axis=1)
    lam = jnp.stack([ssm_lam_re.reshape(depth, D_STATE), ssm_lam_im.reshape(depth, D_STATE),
                     jnp.repeat(ssm_log_dt, SSM_STATE, axis=-1)], axis=1)
    weights = (
        attn_sink,
        norm_g.reshape(depth, 1, D_MODEL), w_in.astype(BF16), lam, bbd, wc,
        ssm_d.reshape(depth, 1, D_SSM), w_glu.astype(BF16), ret_norm_g.reshape(depth, 1, D_RET),
        w_out.astype(BF16), w_ple_proj.astype(BF16), w_ple_gate.astype(BF16),
        final_norm_g.reshape(1, D_MODEL),
    )

    tc_p = WINDOW
    rope_p = _rope_tables(0.0, seq)
    dmask_p, rdec_p = _retention_tables(tc_p, tc_p, GROUPS_PROMPT[1])
    zk = jnp.zeros((bp, WINDOW, LANES), F32)
    zs = jnp.zeros((bp, D_STATE), F32)
    zr = jnp.zeros((bp, N_RET_HEADS * HEAD_DIM, HEAD_DIM), F32)

    tcp_s = 8
    bblk_s = 16
    rope_s = _rope_tables(float(past_len), tcp_s)
    dmask_s, rdec_s = _retention_tables(dec_seq, tcp_s, GROUPS_SAMPLE[1])
    pad_t = ((0, 0), (0, tcp_s - dec_seq), (0, 0))
    hs = jnp.pad(x_sample, pad_t)
    ps = jnp.pad(p_sample, ((0, 0),) + pad_t)
    kwin_s = cache_win_k.transpose(0, 1, 3, 4, 2).reshape(depth, bs, LANES, WINDOW)
    vwin_s = cache_win_v.transpose(0, 1, 3, 4, 2).reshape(depth, bs, LANES, WINDOW)
    sre_s = state_ssm_re.reshape(depth, bs, D_STATE)
    sim_s = state_ssm_im.reshape(depth, bs, D_STATE)
    r_s = state_ret.reshape(depth, bs, N_RET_HEADS * HEAD_DIM, HEAD_DIM)

    hp = x_prompt
    outs_p = []
    for i in range(depth):
        cfg_p = (bp, tc_p, tc_p, seq // tc_p, 0, i, 1, i == depth - 1) + GROUPS_PROMPT
        hp, *st = _layer_call(cfg_p, bp, hp, p_prompt, rope_p, dmask_p, rdec_p, zk, zk, zs, zs, zr, weights)
        outs_p.append(st)
    st_p = [jnp.concatenate(t, axis=0) for t in zip(*outs_p)]
    cfg_s = (bblk_s, dec_seq, tcp_s, 1, past_len, 0, depth, True) + GROUPS_SAMPLE
    hs, *st_s = _layer_call(cfg_s, bs, hs, ps, rope_s, dmask_s, rdec_s, kwin_s, vwin_s, sre_s, sim_s, r_s, weights)

    def pack(st, b, transposed):
        k, v, sr, si, r = st
        if transposed:
            k, v = (t.reshape(depth, b, N_KV_HEADS, HEAD_DIM, WINDOW).transpose(0, 1, 4, 2, 3) for t in (k, v))
        return (k.reshape(depth, b, WINDOW, N_KV_HEADS, HEAD_DIM), v.reshape(depth, b, WINDOW, N_KV_HEADS, HEAD_DIM),
                sr.reshape(depth, b, N_SSM_GROUPS, SSM_STATE), si.reshape(depth, b, N_SSM_GROUPS, SSM_STATE),
                r.reshape(depth, b, N_RET_HEADS, HEAD_DIM, HEAD_DIM))

    return (hp, hs) + pack(st_p, bp, False) + pack(st_s, bs, True)
```
